```python
import jax, jax.numpy as jnp
from jax import lax
import numpy as np

D_MODEL = 1024
BATCH = 8
SEQ = 4096
DEPTH = 4

N_MIXERS = 3
N_META = 16
NORM_EPS = 1e-6

FOX_HEADS = 16
FOX_HEAD_DIM = D_MODEL // FOX_HEADS
FOX_Q_BLOCK = 128
FOX_IN = 4 * D_MODEL + FOX_HEADS

GLA_HEADS = 4
GLA_DK = D_MODEL // 2 // GLA_HEADS
GLA_DV = D_MODEL // GLA_HEADS
GLA_GATE_RANK = 16
GLA_GATE_NORMALIZER = 16.0
GLA_CHUNK = 64
GLA_QK = GLA_HEADS * GLA_DK
GLA_V = GLA_HEADS * GLA_DV
GLA_IN = 2 * GLA_QK + 2 * GLA_V + GLA_GATE_RANK

GDN_HEADS = 8
GDN_DK = 128
GDN_DV = 128
GDN_CONV = 4
GDN_CHUNK = 64
GDN_CONV_DIM = 2 * GDN_HEADS * GDN_DK + GDN_HEADS * GDN_DV
GDN_IN = GDN_CONV_DIM + GDN_HEADS * GDN_DV + 2 * GDN_HEADS

D_FF = ((-(-8 * D_MODEL // 3) + 255) // 256) * 256

N_FOX = (DEPTH + 2) // 3
N_GLA = (DEPTH + 1) // 3
N_GDN = DEPTH // 3

kernel_name = "fox_gla_gdn_interleaved_hybrid"


def rmsnorm(x, g):
    xf = x.astype(jnp.float32)
    y = xf * lax.rsqrt(jnp.mean(xf * xf, axis=-1, keepdims=True) + NORM_EPS)
    return (y * g.astype(jnp.float32)).astype(x.dtype)


def l2norm(x):
    xf = x.astype(jnp.float32)
    return (xf * lax.rsqrt(jnp.sum(xf * xf, axis=-1, keepdims=True) + NORM_EPS)).astype(x.dtype)


def to_chunks(a, chunk):
    b, t = a.shape[0], a.shape[1]
    return jnp.moveaxis(a.reshape((b, t // chunk, chunk) + a.shape[2:]), 1, 0)


def from_chunks(a):
    a = jnp.moveaxis(a, 0, 1)
    return a.reshape((a.shape[0], a.shape[1] * a.shape[2]) + a.shape[3:])


def chunked_scan(chunk_fn, state0, seqs, chunk):
    state, o_meta = chunk_fn(state0, tuple(a[:, :N_META] for a in seqs))
    real = tuple(to_chunks(a[:, N_META:], chunk) for a in seqs)
    _, o_real = lax.scan(chunk_fn, state, real)
    return jnp.concatenate([o_meta, from_chunks(o_real)], axis=1)


def fox_attend(q, cq, qpos, k, v, ck, kpos):
    s = jnp.einsum('bqhd,bkhd->bhqk', q, k).astype(jnp.float32)
    bias = cq[..., :, None] - ck[..., None, :]
    mask = kpos[None, :] <= qpos[:, None]
    s = jnp.where(mask, s + bias, -jnp.inf)
    p = jax.nn.softmax(s, axis=-1)
    return jnp.einsum('bhqk,bkhd->bqhd', p.astype(v.dtype), v)


def fox_mixer(h, w_in, b_f, q_gain, k_gain, w_out):
    B, L, _ = h.shape
    H, Dh = FOX_HEADS, FOX_HEAD_DIM
    proj = h @ w_in
    q, k, v, gate, f = jnp.split(proj, [D_MODEL, 2 * D_MODEL, 3 * D_MODEL, 4 * D_MODEL], axis=-1)
    q = rmsnorm(q.reshape(B, L, H, Dh), q_gain) * (Dh ** -0.5)
    k = rmsnorm(k.reshape(B, L, H, Dh), k_gain)
    v = v.reshape(B, L, H, Dh)
    log_f = jax.nn.log_sigmoid((f + b_f).astype(jnp.float32))
    c = jnp.cumsum(log_f, axis=1).transpose(0, 2, 1)
    kpos = jnp.arange(L)

    o_meta = fox_attend(q[:, :N_META], c[:, :, :N_META], kpos[:N_META],
                        k[:, :N_META], v[:, :N_META], c[:, :, :N_META], kpos[:N_META])

    n_blk = (L - N_META) // FOX_Q_BLOCK
    q_blocks = q[:, N_META:].reshape(B, n_blk, FOX_Q_BLOCK, H, Dh).swapaxes(0, 1)
    cq_blocks = c[:, :, N_META:].reshape(B, H, n_blk, FOX_Q_BLOCK).transpose(2, 0, 1, 3)
    qpos_blocks = (N_META + jnp.arange(n_blk * FOX_Q_BLOCK)).reshape(n_blk, FOX_Q_BLOCK)

    def block(args):
        qb, cqb, qp = args
        return fox_attend(qb, cqb, qp, k, v, c, kpos)

    o_real = from_chunks(lax.map(block, (q_blocks, cq_blocks, qpos_blocks)))
    o = jnp.concatenate([o_meta, o_real], axis=1).reshape(B, L, H * Dh)
    o = o * jax.nn.sigmoid(gate)
    return (o @ w_out).astype(h.dtype)


def gla_chunk(state, inputs):
    q, k, v, g = inputs
    C = q.shape[1]
    b = jnp.cumsum(g, axis=1)
    causal = jnp.tril(jnp.ones((C, C), dtype=bool))
    diff = b[:, :, None] - b[:, None, :]
    decay = jnp.exp(jnp.where(causal[None, :, :, None, None], diff, -jnp.inf))
    A = jnp.einsum('bthd,bshd,btshd->bhts', q, k, decay)
    o_intra = jnp.einsum('bhts,bshv->bthv', A, v)
    o_inter = jnp.einsum('bthd,bhdv->bthv', q * jnp.exp(b), state)
    b_last = b[:, -1]
    k_dec = k * jnp.exp(b_last[:, None] - b)
    new_state = state * jnp.exp(b_last)[..., None] + jnp.einsum('bshd,bshv->bhdv', k_dec, v)
    return new_state, (o_intra + o_inter).astype(v.dtype)


def gla_mixer(h, w_in, w_alpha2, b_alpha, o_gain, w_out):
    B, L, _ = h.shape
    H = GLA_HEADS
    proj = h @ w_in
    q, k, v, r, a_lr = jnp.split(proj, [GLA_QK, 2 * GLA_QK, 2 * GLA_QK + GLA_V, 2 * GLA_QK + 2 * GLA_V], axis=-1)
    q = q.reshape(B, L, H, GLA_DK) * (GLA_DK ** -0.5)
    k = k.reshape(B, L, H, GLA_DK)
    v = v.reshape(B, L, H, GLA_DV)
    g = jax.nn.log_sigmoid((a_lr @ w_alpha2 + b_alpha).astype(jnp.float32)) / GLA_GATE_NORMALIZER
    g = g.reshape(B, L, H, GLA_DK)
    state0 = jnp.zeros((B, H, GLA_DK, GLA_DV), jnp.float32)
    o = chunked_scan(gla_chunk, state0, (q, k, v, g), GLA_CHUNK)
    o = rmsnorm(o, o_gain) * jax.nn.silu(r.reshape(B, L, H, GLA_DV))
    return (o.reshape(B, L, GLA_V) @ w_out).astype(h.dtype)


def causal_depthwise_conv(x, w):
    return lax.conv_general_dilated(
        x, w.astype(x.dtype), window_strides=(1,), padding=[(GDN_CONV - 1, 0)],
        dimension_numbers=('NWC', 'WIO', 'NWC'), feature_group_count=x.shape[-1])


def gdn_chunk(state, inputs):
    q, k, v, g, beta = inputs
    C = q.shape[1]
    qh = q.transpose(0, 2, 1, 3)
    kh = k.transpose(0, 2, 1, 3)
    vh = v.transpose(0, 2, 1, 3)
    bt = beta.transpose(0, 2, 1).astype(jnp.float32)[..., None]
    b = jnp.cumsum(g, axis=1).transpose(0, 2, 1)
    diff = b[..., :, None] - b[..., None, :]
    incl = jnp.tril(jnp.ones((C, C), dtype=bool))
    strict = jnp.tril(jnp.ones((C, C), dtype=bool), k=-1)
    kb = kh.astype(jnp.float32) * bt
    vb = vh.astype(jnp.float32) * bt
    lower = jnp.einsum('bhtd,bhsd->bhts', kb, kh) * jnp.exp(jnp.where(strict, diff, -jnp.inf))
    t_mat = jnp.eye(C, dtype=jnp.float32) + lower
    rhs = jnp.concatenate([vb, kb * jnp.exp(b)[..., None]], axis=-1)
    sol = lax.linalg.triangular_solve(t_mat, rhs, left_side=True, lower=True, unit_diagonal=True)
    u, w = sol[..., :GDN_DV], sol[..., GDN_DV:]
    v_new = u - jnp.einsum('bhcd,bhdv->bhcv', w, state)
    attn = jnp.einsum('bhtd,bhsd->bhts', qh, kh) * jnp.exp(jnp.where(incl, diff, -jnp.inf))
    o = jnp.einsum('bhts,bhsv->bhtv', attn, v_new) + \
        jnp.einsum('bhtd,bhdv->bhtv', qh * jnp.exp(b)[..., None], state)
    b_last = b[..., -1]
    k_dec = kh * jnp.exp(b_last[..., None] - b)[..., None]
    new_state = state * jnp.exp(b_last)[..., None, None] + jnp.einsum('bhsd,bhsv->bhdv', k_dec, v_new)
    return new_state, o.transpose(0, 2, 1, 3).astype(v.dtype)


def gdn_mixer(h, w_in, conv_w, a_log, dt_bias, o_gain, w_out):
    B, L, _ = h.shape
    H = GDN_HEADS
    proj = h @ w_in
    qkv, gate, a, beta_logit = jnp.split(
        proj, [GDN_CONV_DIM, GDN_CONV_DIM + H * GDN_DV, GDN_CONV_DIM + H * GDN_DV + H], axis=-1)
    qkv = jax.nn.silu(causal_depthwise_conv(qkv, conv_w))
    q, k, v = jnp.split(qkv, [H * GDN_DK, 2 * H * GDN_DK], axis=-1)
    q = l2norm(q.reshape(B, L, H, GDN_DK)) * (GDN_DK ** -0.5)
    k = l2norm(k.reshape(B, L, H, GDN_DK))
    v = v.reshape(B, L, H, GDN_DV)
    beta = jax.nn.sigmoid(beta_logit)
    g = -jnp.exp(a_log.astype(jnp.float32)) * jax.nn.softplus((a + dt_bias).astype(jnp.float32))
    state0 = jnp.zeros((B, H, GDN_DK, GDN_DV), jnp.float32)
    o = chunked_scan(gdn_chunk, state0, (q, k, v, g, beta), GDN_CHUNK)
    o = rmsnorm(o, o_gain) * jax.nn.silu(gate.reshape(B, L, H, GDN_DV))
    return (o.reshape(B, L, H * GDN_DV) @ w_out).astype(h.dtype)


def swiglu(h, w_gate_up, w_down):
    gu = h @ w_gate_up
    gt, up = gu[..., :D_FF], gu[..., D_FF:]
    return ((jax.nn.silu(gt) * up) @ w_down).astype(h.dtype)


def setup_inputs(seed: int = 0) -> dict:
    key = jax.random.key(seed)
    ks = jax.random.split(key, 24)

    def nrm(k, shape, fan_in):
        return jax.random.normal(k, shape, jnp.float32) * (fan_in ** -0.5)

    def gain(k, shape):
        return 1.0 + 0.02 * jax.random.normal(k, shape, jnp.float32)

    dt = jnp.exp(jax.random.uniform(ks[20], (N_GDN, GDN_HEADS), jnp.float32, np.log(1e-3), np.log(1e-1)))
    return {
        "x": jax.random.normal(ks[0], (BATCH, SEQ, D_MODEL), jnp.float32),
        "meta_tokens": jax.random.normal(ks[1], (N_META, D_MODEL), jnp.float32),
        "norm_mix": gain(ks[2], (DEPTH, D_MODEL)),
        "norm_ffn": gain(ks[3], (DEPTH, D_MODEL)),
        "w_gate_up": nrm(ks[4], (DEPTH, D_MODEL, 2 * D_FF), D_MODEL),
        "w_down": nrm(ks[5], (DEPTH, D_FF, D_MODEL), D_FF),
        "fox_w_in": nrm(ks[6], (N_FOX, D_MODEL, FOX_IN), D_MODEL),
        "fox_b_f": 2.0 + 0.5 * jax.random.normal(ks[7], (N_FOX, FOX_HEADS), jnp.float32),
        "fox_q_gain": gain(ks[8], (N_FOX, FOX_HEAD_DIM)),
        "fox_k_gain": gain(ks[9], (N_FOX, FOX_HEAD_DIM)),
        "fox_w_out": nrm(ks[10], (N_FOX, D_MODEL, D_MODEL), D_MODEL),
        "gla_w_in": nrm(ks[11], (N_GLA, D_MODEL, GLA_IN), D_MODEL),
        "gla_w_alpha2": nrm(ks[12], (N_GLA, GLA_GATE_RANK, GLA_QK), GLA_GATE_RANK),
        "gla_b_alpha": 0.1 * jax.random.normal(ks[13], (N_GLA, GLA_QK), jnp.float32),
        "gla_o_gain": gain(ks[14], (N_GLA, GLA_DV)),
        "gla_w_out": nrm(ks[15], (N_GLA, GLA_V, D_MODEL), GLA_V),
        "gdn_w_in": nrm(ks[16], (N_GDN, D_MODEL, GDN_IN), D_MODEL),
        "gdn_conv_w": nrm(ks[17], (N_GDN, GDN_CONV, 1, GDN_CONV_DIM), GDN_CONV),
        "gdn_a_log": jnp.log(jax.random.uniform(ks[18], (N_GDN, GDN_HEADS), jnp.float32, 1.0, 16.0)),
        "gdn_dt_bias": dt + jnp.log(-jnp.expm1(-dt)),
        "gdn_o_gain": gain(ks[19], (N_GDN, GDN_DV)),
        "gdn_w_out": nrm(ks[21], (N_GDN, GDN_HEADS * GDN_DV, D_MODEL), GDN_HEADS * GDN_DV),
    }


def reference(x, meta_tokens, norm_mix, norm_ffn, w_gate_up, w_down,
              fox_w_in, fox_b_f, fox_q_gain, fox_k_gain, fox_w_out,
              gla_w_in, gla_w_alpha2, gla_b_alpha, gla_o_gain, gla_w_out,
              gdn_w_in, gdn_conv_w, gdn_a_log, gdn_dt_bias, gdn_o_gain, gdn_w_out):
    B = x.shape[0]
    meta = jnp.broadcast_to(meta_tokens[None].astype(x.dtype), (B, N_META, D_MODEL))
    h = jnp.concatenate([meta, x], axis=1)
    for i in range(DEPTH):
        kind, j = i % N_MIXERS, i // N_MIXERS
        y = rmsnorm(h, norm_mix[i])
        if kind == 0:
            mix = fox_mixer(y, fox_w_in[j], fox_b_f[j], fox_q_gain[j], fox_k_gain[j], fox_w_out[j])
        elif kind == 1:
            mix = gla_mixer(y, gla_w_in[j], gla_w_alpha2[j], gla_b_alpha[j], gla_o_gain[j], gla_w_out[j])
        else:
            mix = gdn_mixer(y, gdn_w_in[j], gdn_conv_w[j], gdn_a_log[j], gdn_dt_bias[j], gdn_o_gain[j], gdn_w_out[j])
        h = h + mix
        h = h + swiglu(rmsnorm(h, norm_ffn[i]), w_gate_up[i], w_down[i])
    return h[:, N_META:]
```

```python
import functools

import jax
import jax.numpy as jnp
from jax import lax
from jax.experimental import pallas as pl
from jax.experimental.pallas import tpu as pltpu

f32 = jnp.float32
bf16 = jnp.bfloat16

D_MODEL = 1024
N_META = 16
NORM_EPS = 1e-6
LANES = 128
SUB_BLOCK = 16

FOX_HEADS = 16
FOX_HEAD_DIM = 64
FOX_BLOCK = 256

GLA_HEADS = 4
GLA_DK = 128
GLA_DV = 256
GLA_QK = GLA_HEADS * GLA_DK
GLA_V = GLA_HEADS * GLA_DV
GLA_GATE_NORMALIZER = 16.0
GLA_CHUNK = 64

GDN_HEADS = 8
GDN_DK = 128
GDN_DV = 128
GDN_CONV = 4
GDN_CHUNK = 64
GDN_QKV = 3 * GDN_HEADS * GDN_DK

D_FF = 2816
FFN_CHUNK = 256

ROW_TILE = 512
VMEM_LIMIT = 56 * 2 ** 20

_HI = lax.Precision.HIGHEST


def _params(*sem):
    return pltpu.CompilerParams(dimension_semantics=sem, vmem_limit_bytes=VMEM_LIMIT)


def _resident(shape):
    nd = len(shape)
    return pl.BlockSpec(shape, lambda *_: (0,) * nd, pipeline_mode=pl.Buffered(1))


def _dot(a, b, precision=None):
    return jnp.dot(a, b, preferred_element_type=f32, precision=precision)


def _dot_nt(a, b, precision=None):
    return lax.dot_general(a, b, (((1,), (1,)), ((), ())), preferred_element_type=f32,
                           precision=precision)


def _dot_tn(a, b, precision=None):
    return lax.dot_general(a, b, (((0,), (0,)), ((), ())), preferred_element_type=f32,
                           precision=precision)


def _sigmoid(x):
    return 1.0 / (1.0 + jnp.exp(-x))


def _softplus(x):
    return jnp.maximum(x, 0.0) + jnp.log1p(jnp.exp(-jnp.abs(x)))


def _log_sigmoid(x):
    return -_softplus(-x)


def _rmsnorm(x, g):
    return x * lax.rsqrt(jnp.mean(x * x, axis=-1, keepdims=True) + NORM_EPS) * g


def _tril(n, k=0):
    r = lax.broadcasted_iota(jnp.int32, (n, n), 0)
    c = lax.broadcasted_iota(jnp.int32, (n, n), 1)
    return r + k >= c


def _norm_proj_kernel(h_ref, g_ref, w_ref, ws_ref, o_ref, os_ref, *, tn):
    y = _rmsnorm(h_ref[...], g_ref[...]).astype(bf16)
    os_ref[...] = _dot(y, ws_ref[...])
    for j in range(w_ref.shape[1] // tn):
        o_ref[:, j * tn:(j + 1) * tn] = _dot(y, w_ref[:, j * tn:(j + 1) * tn]).astype(bf16)


def _norm_proj(h, gain, w_main, w_small):
    m = h.shape[0]
    tm = min(ROW_TILE, m)
    n = w_main.shape[1]
    return pl.pallas_call(
        functools.partial(_norm_proj_kernel, tn=512),
        grid=(m // tm,),
        in_specs=[pl.BlockSpec((tm, D_MODEL), lambda i: (i, 0)),
                  _resident((1, D_MODEL)), _resident(w_main.shape), _resident(w_small.shape)],
        out_specs=[pl.BlockSpec((tm, n), lambda i: (i, 0)),
                   pl.BlockSpec((tm, LANES), lambda i: (i, 0))],
        out_shape=[jax.ShapeDtypeStruct((m, n), bf16), jax.ShapeDtypeStruct((m, LANES), f32)],
        compiler_params=_params("parallel"),
        name="norm_proj",
    )(h, gain.reshape(1, D_MODEL), w_main, w_small)


def _mix_ffn_kernel(h_ref, o_ref, wo_ref, g_ref, wgu_ref, wd_ref, out_ref):
    hmid = h_ref[...] + _dot(o_ref[...], wo_ref[...])
    y = _rmsnorm(hmid, g_ref[...]).astype(bf16)
    out_ref[...] = hmid
    for f in range(D_FF // FFN_CHUNK):
        lo = f * FFN_CHUNK
        gt = _dot(y, wgu_ref[:, lo:lo + FFN_CHUNK])
        up = _dot(y, wgu_ref[:, D_FF + lo:D_FF + lo + FFN_CHUNK])
        act = (gt * _sigmoid(gt) * up).astype(bf16)
        out_ref[...] += _dot(act, wd_ref[lo:lo + FFN_CHUNK, :])


def _mix_ffn(h, o, w_out, gain, w_gate_up, w_down):
    m = h.shape[0]
    tm = min(ROW_TILE, m)
    row = lambda i: (i, 0)
    return pl.pallas_call(
        _mix_ffn_kernel,
        grid=(m // tm,),
        in_specs=[pl.BlockSpec((tm, D_MODEL), row), pl.BlockSpec((tm, D_MODEL), row),
                  _resident(w_out.shape), _resident((1, D_MODEL)),
                  _resident(w_gate_up.shape), _resident(w_down.shape)],
        out_specs=pl.BlockSpec((tm, D_MODEL), row),
        out_shape=jax.ShapeDtypeStruct((m, D_MODEL), f32),
        compiler_params=_params("parallel"),
        name="mix_ffn",
    )(h, o, w_out, gain.reshape(1, D_MODEL), w_gate_up, w_down)


def _fox_prep_kernel(q_ref, k_ref, f_ref, bf_ref, qg_ref, kg_ref, c0_ref,
                     qn_ref, kn_ref, crow_ref, carry_ref, *, tp):
    @pl.when(pl.program_id(1) == 0)
    def _():
        carry_ref[...] = c0_ref[...]

    lo = lax.broadcasted_iota(jnp.int32, (tp, LANES), 1) < FOX_HEAD_DIM

    def head_norm(x, g):
        x2 = x * x
        s_lo = jnp.sum(jnp.where(lo, x2, 0.0), axis=-1, keepdims=True)
        s_hi = jnp.sum(jnp.where(lo, 0.0, x2), axis=-1, keepdims=True)
        ms = jnp.where(lo, s_lo, s_hi) * (1.0 / FOX_HEAD_DIM)
        return x * lax.rsqrt(ms + NORM_EPS) * g

    qg = qg_ref[...] * (FOX_HEAD_DIM ** -0.5)
    kg = kg_ref[...]
    for j in range(D_MODEL // LANES):
        sl = slice(j * LANES, (j + 1) * LANES)
        qn_ref[:, sl] = head_norm(q_ref[:, sl].astype(f32), qg).astype(bf16)
        kn_ref[:, sl] = head_norm(k_ref[:, sl].astype(f32), kg).astype(bf16)

    log_f = _log_sigmoid(f_ref[...] + bf_ref[...])
    cum = _dot(_tril(tp).astype(f32), log_f, _HI) + carry_ref[...]
    carry_ref[...] = cum[tp - 1:tp, :]
    crow_ref[...] = cum.T[:FOX_HEADS, :]


def _fox_prep(main, small, b_f, q_gain, k_gain, c0, tp):
    nb, s, _ = main.shape
    blk = lambda col: pl.BlockSpec((None, tp, D_MODEL), lambda b, i: (b, i, col))
    return pl.pallas_call(
        functools.partial(_fox_prep_kernel, tp=tp),
        grid=(nb, s // tp),
        in_specs=[blk(0), blk(1), pl.BlockSpec((None, tp, LANES), lambda b, i: (b, i, 0)),
                  _resident((1, LANES)), _resident((1, LANES)), _resident((1, LANES)),
                  _resident((1, LANES))],
        out_specs=[blk(0), blk(0),
                   pl.BlockSpec((None, None, FOX_HEADS, tp), lambda b, i: (b, i, 0, 0))],
        out_shape=[jax.ShapeDtypeStruct((nb, s, D_MODEL), bf16),
                   jax.ShapeDtypeStruct((nb, s, D_MODEL), bf16),
                   jax.ShapeDtypeStruct((nb, s // tp, FOX_HEADS, tp), f32)],
        scratch_shapes=[pltpu.VMEM((1, LANES), f32)],
        compiler_params=_params("parallel", "arbitrary"),
        name="fox_prep",
    )(main, main, small, b_f, q_gain, k_gain, c0)


def _fox_attn_kernel(q_ref, k_ref, v_ref, km_ref, vm_ref, crow_ref, cm_ref, gate_ref, o_ref,
                     m_ref, l_ref, acc_ref, *, bq, meta_only):
    hp = pl.program_id(1)
    qi = pl.program_id(2)
    q2 = q_ref[...]
    lo = lax.broadcasted_iota(jnp.int32, (1, LANES), 1) < FOX_HEAD_DIM
    zero = jnp.zeros_like(q2)
    qs = (jnp.where(lo, q2, zero), jnp.where(lo, zero, q2))
    eye = (lax.broadcasted_iota(jnp.int32, (bq, bq), 0)
           == lax.broadcasted_iota(jnp.int32, (bq, bq), 1))
    causal = _tril(bq)

    cqs = []
    for h in range(2):
        cq_row = crow_ref[qi, pl.ds(2 * hp + h, 1), :]
        cqs.append(jnp.sum(jnp.where(eye, cq_row, 0.0), axis=1, keepdims=True))

    km = km_ref[...]
    vm = vm_ref[...]
    for h in range(2):
        s = _dot_nt(qs[h], km) + cqs[h] - cm_ref[pl.ds(2 * hp + h, 1), :]
        if meta_only:
            s = jnp.where(causal, s, -jnp.inf)
        m = jnp.max(s, axis=1, keepdims=True)
        p = jnp.exp(s - m)
        m_ref[h] = m
        l_ref[h] = jnp.sum(p, axis=1, keepdims=True)
        acc_ref[h] = _dot(p.astype(bf16), vm)

    if not meta_only:
        def step(j, masked):
            start = pl.multiple_of(j * bq, bq)
            k2 = k_ref[pl.ds(start, bq), :]
            v2 = v_ref[pl.ds(start, bq), :]
            for h in range(2):
                s = _dot_nt(qs[h], k2) + cqs[h] - crow_ref[j, pl.ds(2 * hp + h, 1), :]
                if masked:
                    s = jnp.where(causal, s, -jnp.inf)
                m_old = m_ref[h]
                m_new = jnp.maximum(m_old, jnp.max(s, axis=1, keepdims=True))
                alpha = jnp.exp(m_old - m_new)
                p = jnp.exp(s - m_new)
                l_ref[h] = alpha * l_ref[h] + jnp.sum(p, axis=1, keepdims=True)
                acc_ref[h] = alpha * acc_ref[h] + _dot(p.astype(bf16), v2)
                m_ref[h] = m_new

        def body(j, carry):
            step(j, False)
            return carry

        lax.fori_loop(0, qi, body, 0)
        step(qi, True)

    o = jnp.where(lo, acc_ref[0] / l_ref[0], acc_ref[1] / l_ref[1])
    o_ref[...] = (o * _sigmoid(gate_ref[...].astype(f32))).astype(bf16)


def _fox_attn(qn, kn, main, kn_meta, main_meta, crow, crow_meta, meta_only):
    nb, s, _ = qn.shape
    bq = min(FOX_BLOCK, s)
    pairs = FOX_HEADS // 2
    vcol, gcol = 2 * pairs, 3 * pairs
    cm = crow_meta.reshape(FOX_HEADS, N_META)
    return pl.pallas_call(
        functools.partial(_fox_attn_kernel, bq=bq, meta_only=meta_only),
        grid=(nb, pairs, s // bq),
        in_specs=[pl.BlockSpec((None, bq, LANES), lambda b, p, i: (b, i, p)),
                  pl.BlockSpec((None, s, LANES), lambda b, p, i: (b, 0, p)),
                  pl.BlockSpec((None, s, LANES), lambda b, p, i: (b, 0, vcol + p)),
                  pl.BlockSpec((N_META, LANES), lambda b, p, i: (0, p)),
                  pl.BlockSpec((N_META, LANES), lambda b, p, i: (0, vcol + p)),
                  pl.BlockSpec((None, s // bq, FOX_HEADS, bq), lambda b, p, i: (b, 0, 0, 0)),
                  pl.BlockSpec((FOX_HEADS, N_META), lambda b, p, i: (0, 0)),
                  pl.BlockSpec((None, bq, LANES), lambda b, p, i: (b, i, gcol + p))],
        out_specs=pl.BlockSpec((None, bq, LANES), lambda b, p, i: (b, i, p)),
        out_shape=jax.ShapeDtypeStruct((nb, s, D_MODEL), bf16),
        scratch_shapes=[pltpu.VMEM((2, bq, 1), f32), pltpu.VMEM((2, bq, 1), f32),
                        pltpu.VMEM((2, bq, LANES), f32)],
        compiler_params=_params("parallel", "parallel", "arbitrary"),
        name="fox_attn",
    )(qn, kn, main, kn_meta, main_meta, crow, cm, main)


def _gla_kernel(q_ref, k_ref, v_ref, r_ref, a_ref, w2_ref, ba_ref, og_ref, s0_ref,
                o_ref, sout_ref, st_ref, b_sc, k_sc, *, C):
    c = pl.program_id(1)

    @pl.when(c == 0)
    def _():
        st_ref[...] = s0_ref[...]

    g = _log_sigmoid(_dot(a_ref[...], w2_ref[...], _HI) + ba_ref[...]) * (1.0 / GLA_GATE_NORMALIZER)
    b_sc[...] = _dot(_tril(C).astype(f32), g, _HI)
    k_sc[...] = k_ref[...].astype(f32)

    nb = C // SUB_BLOCK
    trow = lax.broadcasted_iota(jnp.int32, (SUB_BLOCK, 1), 0)
    lane = lax.broadcasted_iota(jnp.int32, (SUB_BLOCK, SUB_BLOCK), 1)

    for h in range(GLA_HEADS):
        sl = slice(h * GLA_DK, (h + 1) * GLA_DK)
        vs = slice(h * GLA_DV, (h + 1) * GLA_DV)
        bh = b_sc[:, sl]
        qh = q_ref[:, sl].astype(f32) * (GLA_DK ** -0.5)
        kh = k_sc[:, sl]
        vh = v_ref[:, vs]
        blast = bh[C - 1:C, :]
        st = st_ref[h]
        o_inter = _dot_nt((qh * jnp.exp(bh)).astype(bf16), st.astype(bf16))

        rows = []
        for i in range(nb):
            r0 = i * SUB_BLOCK
            bi = bh[r0:r0 + SUB_BLOCK]
            qi = qh[r0:r0 + SUB_BLOCK]

            ad = jnp.zeros((SUB_BLOCK, SUB_BLOCK), f32)
            for s in range(SUB_BLOCK):
                bs = b_sc[r0 + s:r0 + s + 1, sl]
                ks = k_sc[r0 + s:r0 + s + 1, sl]
                e = jnp.exp(jnp.where(trow >= s, bi - bs, -jnp.inf))
                col = jnp.sum(qi * ks * e, axis=-1, keepdims=True)
                ad = jnp.where(lane == s, col, ad)
            o_i = _dot(ad.astype(bf16), vh[r0:r0 + SUB_BLOCK])
            if i > 0:
                ref = bi[0:1, :]
                q_rel = (qi * jnp.exp(bi - ref)).astype(bf16)
                k_rel = (kh[:r0] * jnp.exp(ref - bh[:r0])).astype(bf16)
                o_i = o_i + _dot(_dot_nt(q_rel, k_rel).astype(bf16), vh[:r0])
            rows.append(o_i)
        o = jnp.concatenate(rows, axis=0) + o_inter if nb > 1 else rows[0] + o_inter

        k_dec = (kh * jnp.exp(blast - bh)).astype(bf16)
        st_ref[h] = st * jnp.exp(blast) + _dot_tn(vh, k_dec)

        r = r_ref[:, vs].astype(f32)
        o_ref[:, vs] = (_rmsnorm(o, og_ref[...]) * (r * _sigmoid(r))).astype(bf16)

    @pl.when(c == pl.num_programs(1) - 1)
    def _():
        sout_ref[...] = st_ref[...]


def _gla_mix(main, small, w_alpha2, b_alpha, o_gain, s0, C):
    nb, s, _ = main.shape
    blk = lambda w, col: pl.BlockSpec((None, C, w), lambda b, c: (b, c, col))
    state = (GLA_HEADS, GLA_DV, GLA_DK)
    return pl.pallas_call(
        functools.partial(_gla_kernel, C=C),
        grid=(nb, s // C),
        in_specs=[blk(GLA_QK, 0), blk(GLA_QK, 1), blk(GLA_V, 1), blk(GLA_V, 2),
                  pl.BlockSpec((None, C, LANES), lambda b, c: (b, c, 0)),
                  _resident((LANES, GLA_QK)), _resident((1, GLA_QK)), _resident((1, GLA_DV)),
                  pl.BlockSpec((None,) + state, lambda b, c: (0, 0, 0, 0))],
        out_specs=[blk(GLA_V, 0), pl.BlockSpec((None,) + state, lambda b, c: (b, 0, 0, 0))],
        out_shape=[jax.ShapeDtypeStruct((nb, s, GLA_V), bf16),
                   jax.ShapeDtypeStruct((nb,) + state, f32)],
        scratch_shapes=[pltpu.VMEM(state, f32), pltpu.VMEM((C, GLA_QK), f32),
                        pltpu.VMEM((C, GLA_QK), f32)],
        compiler_params=_params("parallel", "arbitrary"),
        name="gla_mix",
    )(main, main, main, main, small, w_alpha2, b_alpha, o_gain, s0)


def _unit_lower_inverse(lower, C):
    r = lax.broadcasted_iota(jnp.int32, (C, C), 0)
    c = lax.broadcasted_iota(jnp.int32, (C, C), 1)
    eye = (r == c).astype(f32)
    same_block = (r // SUB_BLOCK) == (c // SUB_BLOCK)
    m = jnp.where(same_block, -lower, 0.0)
    p = eye + m
    for _ in range(3):
        m = _dot(m, m, _HI)
        p = p + _dot(p, m, _HI)
    if C == SUB_BLOCK:
        return p
    assert C // SUB_BLOCK <= 4
    e = _dot(p, jnp.where(same_block, 0.0, lower), _HI)
    e2 = _dot(e, e, _HI)
    x = eye - e + e2 - _dot(e, e2, _HI)
    return _dot(x, p, _HI)


def _gdn_kernel(x_ref, gate_ref, sm_ref, cw_ref, alog_ref, dtb_ref, og_ref, tail0_ref, s0_ref,
                o_ref, sout_ref, xbuf, s_ref, *, C):
    c = pl.program_id(1)

    @pl.when(c == 0)
    def _():
        xbuf[0:8, :] = tail0_ref[...]
        s_ref[...] = s0_ref[...]

    xbuf[8:8 + C, :] = x_ref[...].astype(f32)

    def conv_silu(col):
        sl = slice(col, col + LANES)
        y = cw_ref[0:1, sl] * xbuf[5:5 + C, sl]
        for j in range(1, GDN_CONV):
            y = y + cw_ref[j:j + 1, sl] * xbuf[5 + j:5 + j + C, sl]
        return y * _sigmoid(y)

    def l2norm(x):
        return x * lax.rsqrt(jnp.sum(x * x, axis=-1, keepdims=True) + NORM_EPS)

    sm = sm_ref[...]
    g_all = -jnp.exp(alog_ref[...]) * _softplus(sm + dtb_ref[...])
    beta_all = _sigmoid(sm)
    b_all = _dot(_tril(C).astype(f32), g_all, _HI)
    b_rows = b_all.T
    incl = _tril(C)
    strict = _tril(C, -1)
    hd = GDN_HEADS * GDN_DK

    for h in range(GDN_HEADS):
        qh = l2norm(conv_silu(h * GDN_DK)) * (GDN_DK ** -0.5)
        kh = l2norm(conv_silu(hd + h * GDN_DK))
        vh = conv_silu(2 * hd + h * GDN_DV)
        b_col = b_all[:, h:h + 1]
        beta = beta_all[:, GDN_HEADS + h:GDN_HEADS + h + 1]
        diff = b_col - b_rows[h:h + 1, :]
        kb = kh * beta
        vb = vh * beta
        k16 = kh.astype(bf16)
        lower = _dot_nt(kb.astype(bf16), k16) * jnp.exp(jnp.where(strict, diff, -jnp.inf))
        t_inv = _unit_lower_inverse(lower, C)
        e_b = jnp.exp(b_col)
        sol = _dot(t_inv, jnp.concatenate([vb, kb * e_b], axis=1), _HI)
        u, w = sol[:, :GDN_DV], sol[:, GDN_DV:]
        st = s_ref[h]
        st16 = st.astype(bf16)
        v_new = u - _dot(w.astype(bf16), st16)
        vn16 = v_new.astype(bf16)
        attn = _dot_nt(qh.astype(bf16), k16) * jnp.exp(jnp.where(incl, diff, -jnp.inf))
        o = _dot(attn.astype(bf16), vn16) + _dot((qh * e_b).astype(bf16), st16)
        b_last = b_col[C - 1:C, :]
        k_dec = (kh * jnp.exp(b_last - b_col)).astype(bf16)
        s_ref[h] = st * jnp.exp(b_last) + _dot_tn(k_dec, vn16)

        vs = slice(h * GDN_DV, (h + 1) * GDN_DV)
        gt = gate_ref[:, vs].astype(f32)
        o_ref[:, vs] = (_rmsnorm(o, og_ref[...]) * (gt * _sigmoid(gt))).astype(bf16)

    xbuf[0:8, :] = xbuf[C:C + 8, :]

    @pl.when(c == pl.num_programs(1) - 1)
    def _():
        sout_ref[...] = s_ref[...]


def _gdn_mix(main, small, conv_w, a_log, dt_bias, o_gain, tail0, s0, C):
    nb, s, _ = main.shape
    state = (GDN_HEADS, GDN_DK, GDN_DV)
    return pl.pallas_call(
        functools.partial(_gdn_kernel, C=C),
        grid=(nb, s // C),
        in_specs=[pl.BlockSpec((None, C, GDN_QKV), lambda b, c: (b, c, 0)),
                  pl.BlockSpec((None, C, D_MODEL), lambda b, c: (b, c, GDN_QKV // D_MODEL)),
                  pl.BlockSpec((None, C, LANES), lambda b, c: (b, c, 0)),
                  _resident((GDN_CONV, GDN_QKV)), _resident((1, LANES)), _resident((1, LANES)),
                  _resident((1, GDN_DV)), _resident((8, GDN_QKV)),
                  pl.BlockSpec((None,) + state, lambda b, c: (0, 0, 0, 0))],
        out_specs=[pl.BlockSpec((None, C, D_MODEL), lambda b, c: (b, c, 0)),
                   pl.BlockSpec((None,) + state, lambda b, c: (b, 0, 0, 0))],
        out_shape=[jax.ShapeDtypeStruct((nb, s, D_MODEL), bf16),
                   jax.ShapeDtypeStruct((nb,) + state, f32)],
        scratch_shapes=[pltpu.VMEM((8 + C, GDN_QKV), f32), pltpu.VMEM(state, f32)],
        compiler_params=_params("parallel", "arbitrary"),
        name="gdn_mix",
    )(main, main, small, conv_w, a_log, dt_bias, o_gain, tail0, s0)


def _split_in_proj(w_in, n_main):
    w_small = jnp.pad(w_in[:, n_main:], ((0, 0), (0, LANES - (w_in.shape[1] - n_main))))
    return w_in[:, :n_main].astype(bf16), w_small.astype(bf16)


def _lane_row(v, width=LANES, offset=0):
    return jnp.pad(v.astype(f32), (offset, width - offset - v.shape[0])).reshape(1, width)


def kernel(x, meta_tokens, norm_mix, norm_ffn, w_gate_up, w_down, fox_w_in, fox_b_f, fox_q_gain, fox_k_gain, fox_w_out, gla_w_in, gla_w_alpha2, gla_b_alpha, gla_o_gain, gla_w_out, gdn_w_in, gdn_conv_w, gdn_a_log, gdn_dt_bias, gdn_o_gain, gdn_w_out):
    nb, seq, _ = x.shape
    depth = norm_mix.shape[0]
    hr = x.reshape(nb * seq, D_MODEL)
    hm = meta_tokens.astype(x.dtype)

    for i in range(depth):
        kind, j = i % 3, i // 3
        if kind == 0:
            w_main, w_small = _split_in_proj(fox_w_in[j], 4 * D_MODEL)
            w_out = fox_w_out[j]
            main_m, small_m = _norm_proj(hm, norm_mix[i], w_main, w_small)
            main_r, small_r = _norm_proj(hr, norm_mix[i], w_main, w_small)
            main_m = main_m.reshape(1, N_META, -1)
            main_r = main_r.reshape(nb, seq, -1)
            b_f = _lane_row(fox_b_f[j])
            qg = jnp.tile(fox_q_gain[j], 2).reshape(1, LANES)
            kg = jnp.tile(fox_k_gain[j], 2).reshape(1, LANES)
            qn_m, kn_m, crow_m = _fox_prep(main_m, small_m.reshape(1, N_META, LANES), b_f, qg, kg,
                                           jnp.zeros((1, LANES), f32), N_META)
            c0 = _lane_row(crow_m[0, 0, :, N_META - 1])
            qn_r, kn_r, crow_r = _fox_prep(main_r, small_r.reshape(nb, seq, LANES), b_f, qg, kg,
                                           c0, min(FOX_BLOCK, seq))
            kn_m2, main_m2 = kn_m[0], main_m[0]
            o_m = _fox_attn(qn_m, kn_m, main_m, kn_m2, main_m2, crow_m, crow_m, True)
            o_r = _fox_attn(qn_r, kn_r, main_r, kn_m2, main_m2, crow_r, crow_m, False)
        elif kind == 1:
            w_main, w_small = _split_in_proj(gla_w_in[j], 2 * GLA_QK + 2 * GLA_V)
            w_out = gla_w_out[j]
            main_m, small_m = _norm_proj(hm, norm_mix[i], w_main, w_small)
            main_r, small_r = _norm_proj(hr, norm_mix[i], w_main, w_small)
            w2 = jnp.pad(gla_w_alpha2[j], ((0, LANES - gla_w_alpha2.shape[1]), (0, 0)))
            ba = gla_b_alpha[j].reshape(1, GLA_QK)
            og = gla_o_gain[j].reshape(1, GLA_DV)
            s0 = jnp.zeros((1, GLA_HEADS, GLA_DV, GLA_DK), f32)
            o_m, s_m = _gla_mix(main_m.reshape(1, N_META, -1), small_m.reshape(1, N_META, LANES),
                                w2, ba, og, s0, N_META)
            o_r, _ = _gla_mix(main_r.reshape(nb, seq, -1), small_r.reshape(nb, seq, LANES),
                              w2, ba, og, s_m, GLA_CHUNK)
        else:
            w_main, w_small = _split_in_proj(gdn_w_in[j], GDN_QKV + GDN_HEADS * GDN_DV)
            w_out = gdn_w_out[j]
            main_m, small_m = _norm_proj(hm, norm_mix[i], w_main, w_small)
            main_r, small_r = _norm_proj(hr, norm_mix[i], w_main, w_small)
            cw = gdn_conv_w[j].reshape(GDN_CONV, GDN_QKV)
            alog = _lane_row(gdn_a_log[j])
            dtb = _lane_row(gdn_dt_bias[j])
            og = gdn_o_gain[j].reshape(1, GDN_DV)
            s0 = jnp.zeros((1, GDN_HEADS, GDN_DK, GDN_DV), f32)
            o_m, s_m = _gdn_mix(main_m.reshape(1, N_META, -1), small_m.reshape(1, N_META, LANES),
                                cw, alog, dtb, og, jnp.zeros((8, GDN_QKV), f32), s0, N_META)
            tail = main_m[N_META - 8:, :GDN_QKV].astype(f32)
            o_r, _ = _gdn_mix(main_r.reshape(nb, seq, -1), small_r.reshape(nb, seq, LANES),
                              cw, alog, dtb, og, tail, s_m, GDN_CHUNK)

        wo = w_out.astype(bf16)
        wgu = w_gate_up[i].astype(bf16)
        wd = w_down[i].astype(bf16)
        hm = _mix_ffn(hm, o_m.reshape(N_META, D_MODEL), wo, norm_ffn[i], wgu, wd)
        hr = _mix_ffn(hr, o_r.reshape(nb * seq, D_MODEL), wo, norm_ffn[i], wgu, wd)

    return hr.reshape(nb, seq, D_MODEL)
```

```python
import functools
import math

import jax
import jax.numpy as jnp
from jax import lax
from jax.experimental import pallas as pl
from jax.experimental.pallas import tpu as pltpu

f32 = jnp.float32
bf16 = jnp.bfloat16

D_MODEL = 1024
N_META = 16
NORM_EPS = 1e-6
LANES = 128
SUB_BLOCK = 16
LOG2E = math.log2(math.e)

FOX_HEADS = 16
FOX_HEAD_DIM = 64
FOX_BLOCK = 256
FOX_V_ROWS = FOX_HEAD_DIM + 16
FOX_BIAS_LANE = FOX_HEAD_DIM

GLA_HEADS = 4
GLA_DK = 128
GLA_DV = 256
GLA_QK = GLA_HEADS * GLA_DK
GLA_V = GLA_HEADS * GLA_DV
GLA_GATE_NORMALIZER = 16.0
GLA_CHUNK = 64

GDN_HEADS = 8
GDN_DK = 128
GDN_DV = 128
GDN_CONV = 4
GDN_CHUNK = 64
GDN_QKV = 3 * GDN_HEADS * GDN_DK
GDN_GROUP = 4

D_FF = 2816
FFN_CHUNK = 256

ROW_TILE = 512
VMEM_LIMIT = 56 * 2 ** 20

_HI = lax.Precision.HIGHEST


def _params(*sem):
    return pltpu.CompilerParams(dimension_semantics=sem, vmem_limit_bytes=VMEM_LIMIT)


def _resident(shape):
    nd = len(shape)
    return pl.BlockSpec(shape, lambda *_: (0,) * nd, pipeline_mode=pl.Buffered(1))


def _dot(a, b, precision=None):
    return jnp.dot(a, b, preferred_element_type=f32, precision=precision)


def _dot_nt(a, b, precision=None):
    return lax.dot_general(a, b, (((1,), (1,)), ((), ())), preferred_element_type=f32,
                           precision=precision)


def _dot_tn(a, b, precision=None):
    return lax.dot_general(a, b, (((0,), (0,)), ((), ())), preferred_element_type=f32,
                           precision=precision)


def _bdot(a, b):
    return _dot(a.astype(bf16), b.astype(bf16))


def _sigmoid(x):
    return 1.0 / (1.0 + jnp.exp(-x))


def _softplus(x):
    return jnp.maximum(x, 0.0) + jnp.log1p(jnp.exp(-jnp.abs(x)))


def _log_sigmoid(x):
    return -_softplus(-x)


def _rmsnorm(x, g):
    return x * lax.rsqrt(jnp.mean(x * x, axis=-1, keepdims=True) + NORM_EPS) * g


def _tril(n, k=0):
    r = lax.broadcasted_iota(jnp.int32, (n, n), 0)
    c = lax.broadcasted_iota(jnp.int32, (n, n), 1)
    return r + k >= c


def _norm_proj_kernel(h_ref, g_ref, w_ref, ws_ref, o_ref, os_ref, *, tn):
    y = _rmsnorm(h_ref[...], g_ref[...]).astype(bf16)
    os_ref[...] = _dot(y, ws_ref[...])
    for j in range(w_ref.shape[1] // tn):
        o_ref[:, j * tn:(j + 1) * tn] = _dot(y, w_ref[:, j * tn:(j + 1) * tn]).astype(bf16)


def _norm_proj(h, gain, w_main, w_small):
    m = h.shape[0]
    tm = min(ROW_TILE, m)
    n = w_main.shape[1]
    return pl.pallas_call(
        functools.partial(_norm_proj_kernel, tn=512),
        grid=(m // tm,),
        in_specs=[pl.BlockSpec((tm, D_MODEL), lambda i: (i, 0)),
                  _resident((1, D_MODEL)), _resident(w_main.shape), _resident(w_small.shape)],
        out_specs=[pl.BlockSpec((tm, n), lambda i: (i, 0)),
                   pl.BlockSpec((tm, LANES), lambda i: (i, 0))],
        out_shape=[jax.ShapeDtypeStruct((m, n), bf16), jax.ShapeDtypeStruct((m, LANES), f32)],
        compiler_params=_params("parallel"),
        name="norm_proj",
    )(h, gain.reshape(1, D_MODEL), w_main, w_small)


def _mix_ffn_kernel(h_ref, o_ref, wo_ref, g_ref, wgu_ref, wd_ref, out_ref):
    hmid = h_ref[...] + _dot(o_ref[...], wo_ref[...])
    y = _rmsnorm(hmid, g_ref[...]).astype(bf16)
    out_ref[...] = hmid
    for f in range(D_FF // FFN_CHUNK):
        lo = f * FFN_CHUNK
        gt = _dot(y, wgu_ref[:, lo:lo + FFN_CHUNK])
        up = _dot(y, wgu_ref[:, D_FF + lo:D_FF + lo + FFN_CHUNK])
        act = (gt * _sigmoid(gt) * up).astype(bf16)
        out_ref[...] += _dot(act, wd_ref[lo:lo + FFN_CHUNK, :])


def _mix_ffn(h, o, w_out, gain, w_gate_up, w_down):
    m = h.shape[0]
    tm = min(ROW_TILE, m)
    row = lambda i: (i, 0)
    return pl.pallas_call(
        _mix_ffn_kernel,
        grid=(m // tm,),
        in_specs=[pl.BlockSpec((tm, D_MODEL), row), pl.BlockSpec((tm, D_MODEL), row),
                  _resident(w_out.shape), _resident((1, D_MODEL)),
                  _resident(w_gate_up.shape), _resident(w_down.shape)],
        out_specs=pl.BlockSpec((tm, D_MODEL), row),
        out_shape=jax.ShapeDtypeStruct((m, D_MODEL), f32),
        compiler_params=_params("parallel"),
        name="mix_ffn",
    )(h, o, w_out, gain.reshape(1, D_MODEL), w_gate_up, w_down)


def _fox_bias_tables():
    r = jnp.arange(3 * LANES)[:, None]
    c = jnp.arange(FOX_HEADS * LANES)[None, :]
    head, lane = c // LANES, c % LANES
    piece, src_head = r // LANES, r % LANES
    hit = src_head == head
    place_q = (hit & (lane == FOX_BIAS_LANE + 3 + piece)).astype(bf16)
    place_k = (hit & (lane == FOX_BIAS_LANE + piece)).astype(bf16)
    lane1 = jnp.arange(FOX_HEADS * LANES)[None, :] % LANES
    const_q = -((lane1 >= FOX_BIAS_LANE) & (lane1 < FOX_BIAS_LANE + 3)).astype(f32)
    const_k = ((lane1 >= FOX_BIAS_LANE + 3) & (lane1 < FOX_BIAS_LANE + 6)).astype(f32)
    return place_q, place_k, const_q, const_k


def _fox_prep_kernel(q_ref, k_ref, v_ref, f_ref, bf_ref, qg_ref, kg_ref, c0_ref,
                     pq_ref, pk_ref, cq_ref, ck_ref,
                     qx_ref, kx_ref, vx_ref, cend_ref, carry_ref, *, tp):
    @pl.when(pl.program_id(1) == 0)
    def _():
        carry_ref[...] = c0_ref[...]

    log_f = _log_sigmoid(f_ref[...] + bf_ref[...])
    cum = _dot(_tril(tp).astype(f32), log_f, _HI) + carry_ref[...]
    carry_ref[...] = cum[tp - 1:tp, :]
    cend_ref[...] = cum[tp - 1:tp, :]
    c2 = cum * LOG2E
    hi = c2.astype(bf16)
    r1 = c2 - hi.astype(f32)
    mid = r1.astype(bf16)
    low = (r1 - mid.astype(f32)).astype(bf16)
    x3 = jnp.concatenate([hi, mid, low], axis=1)
    ext_q = _dot(x3, pq_ref[...]) + cq_ref[...]
    ext_k = _dot(x3, pk_ref[...]) + ck_ref[...]

    lo = lax.broadcasted_iota(jnp.int32, (tp, LANES), 1) < FOX_HEAD_DIM

    def head_norm(x, g):
        x2 = x * x
        s_lo = jnp.sum(jnp.where(lo, x2, 0.0), axis=-1, keepdims=True)
        s_hi = jnp.sum(jnp.where(lo, 0.0, x2), axis=-1, keepdims=True)
        ms = jnp.where(lo, s_lo, s_hi) * (1.0 / FOX_HEAD_DIM)
        return x * lax.rsqrt(ms + NORM_EPS) * g

    ones = jnp.ones((FOX_V_ROWS - FOX_HEAD_DIM, tp), bf16)
    for p in range(FOX_HEADS // 2):
        sl = slice(p * LANES, (p + 1) * LANES)
        qn = head_norm(q_ref[:, sl].astype(f32), qg_ref[...])
        kn = head_norm(k_ref[:, sl].astype(f32), kg_ref[...])
        vt = v_ref[:, sl].astype(f32).T
        halves = ((qn, kn), (pltpu.roll(qn, FOX_HEAD_DIM, axis=1), pltpu.roll(kn, FOX_HEAD_DIM, axis=1)))
        for hh, (qv, kv) in enumerate(halves):
            h = 2 * p + hh
            hs = slice(h * LANES, (h + 1) * LANES)
            qx_ref[h] = jnp.where(lo, qv, ext_q[:, hs]).astype(bf16)
            kx_ref[h] = jnp.where(lo, kv, ext_k[:, hs]).astype(bf16)
            vx_ref[h, 0:FOX_HEAD_DIM, :] = vt[hh * FOX_HEAD_DIM:(hh + 1) * FOX_HEAD_DIM].astype(bf16)
            vx_ref[h, FOX_HEAD_DIM:FOX_V_ROWS, :] = ones


def _fox_prep(main, small, b_f, q_gain, k_gain, c0, tables, tp):
    nb, s, _ = main.shape
    blk = lambda col: pl.BlockSpec((None, tp, D_MODEL), lambda b, i: (b, i, col))
    head_blk = pl.BlockSpec((None, FOX_HEADS, tp, LANES), lambda b, i: (b, 0, i, 0))
    return pl.pallas_call(
        functools.partial(_fox_prep_kernel, tp=tp),
        grid=(nb, s // tp),
        in_specs=[blk(0), blk(1), blk(2), pl.BlockSpec((None, tp, LANES), lambda b, i: (b, i, 0)),
                  _resident((1, LANES)), _resident((1, LANES)), _resident((1, LANES)),
                  _resident((1, LANES))] + [_resident(t.shape) for t in tables],
        out_specs=[head_blk, head_blk,
                   pl.BlockSpec((None, FOX_HEADS, None, FOX_V_ROWS, tp), lambda b, i: (b, 0, i, 0, 0)),
                   pl.BlockSpec((None, 1, LANES), lambda b, i: (b, 0, 0))],
        out_shape=[jax.ShapeDtypeStruct((nb, FOX_HEADS, s, LANES), bf16),
                   jax.ShapeDtypeStruct((nb, FOX_HEADS, s, LANES), bf16),
                   jax.ShapeDtypeStruct((nb, FOX_HEADS, s // tp, FOX_V_ROWS, tp), bf16),
                   jax.ShapeDtypeStruct((nb, 1, LANES), f32)],
        scratch_shapes=[pltpu.VMEM((1, LANES), f32)],
        compiler_params=_params("parallel", "arbitrary"),
        name="fox_prep",
    )(main, main, main, small, b_f, q_gain, k_gain, c0, *tables)


def _fox_attn_kernel(qx_ref, kx_ref, vx_ref, kxm_ref, vxm_ref, gate_ref, o_ref, m_ref, acc_ref,
                     *, bq, meta_only):
    qi = pl.program_id(2)
    visible = (lax.broadcasted_iota(jnp.int32, (bq, bq), 0)
               <= lax.broadcasted_iota(jnp.int32, (bq, bq), 1))

    def scores(j):
        start = pl.multiple_of(j * bq, bq)
        keep = visible | (j < qi)
        return tuple(jnp.where(keep, _dot_nt(kx_ref[h, pl.ds(start, bq), :], qx_ref[h]), -jnp.inf)
                     for h in range(2))

    first = None if meta_only else scores(0)

    for h in range(2):
        s = _dot_nt(kxm_ref[h], qx_ref[h])
        if meta_only:
            s = jnp.where(visible, s, -jnp.inf)
        m = jnp.max(s, axis=0, keepdims=True)
        m_ref[h] = m
        acc_ref[h] = _dot(vxm_ref[h], jnp.exp2(s - m).astype(bf16))

    if not meta_only:
        def consume(j, ss):
            for h in range(2):
                m_old = m_ref[h]
                m_new = jnp.maximum(m_old, jnp.max(ss[h], axis=0, keepdims=True))
                p = jnp.exp2(ss[h] - m_new).astype(bf16)
                acc_ref[h] = jnp.exp2(m_old - m_new) * acc_ref[h] + _dot(vx_ref[h, j], p)
                m_ref[h] = m_new

        def body(j, ss):
            nxt = scores(j + 1)
            consume(j, ss)
            return nxt

        consume(qi, lax.fori_loop(0, qi, body, first))

    outs = []
    for h in range(2):
        a = acc_ref[h]
        outs.append(a[0:FOX_HEAD_DIM] / a[FOX_HEAD_DIM:FOX_HEAD_DIM + 1])
    o = jnp.concatenate(outs, axis=0).T
    o_ref[...] = (o * _sigmoid(gate_ref[...].astype(f32))).astype(bf16)


def _fox_attn(qx, kx, vx, kx_meta, vx_meta, main, meta_only):
    nb, _, s, _ = qx.shape
    bq = min(FOX_BLOCK, s)
    pairs = FOX_HEADS // 2
    gcol = 3 * pairs
    return pl.pallas_call(
        functools.partial(_fox_attn_kernel, bq=bq, meta_only=meta_only),
        grid=(nb, pairs, s // bq),
        in_specs=[pl.BlockSpec((None, 2, bq, LANES), lambda b, p, i: (b, p, i, 0)),
                  pl.BlockSpec((None, 2, s, LANES), lambda b, p, i: (b, p, 0, 0)),
                  pl.BlockSpec((None, 2, s // bq, FOX_V_ROWS, bq), lambda b, p, i: (b, p, 0, 0, 0)),
                  pl.BlockSpec((None, 2, N_META, LANES), lambda b, p, i: (0, p, 0, 0)),
                  pl.BlockSpec((None, 2, None, FOX_V_ROWS, N_META), lambda b, p, i: (0, p, 0, 0, 0)),
                  pl.BlockSpec((None, bq, LANES), lambda b, p, i: (b, i, gcol + p))],
        out_specs=pl.BlockSpec((None, bq, LANES), lambda b, p, i: (b, i, p)),
        out_shape=jax.ShapeDtypeStruct((nb, s, D_MODEL), bf16),
        scratch_shapes=[pltpu.VMEM((2, 1, bq), f32), pltpu.VMEM((2, FOX_V_ROWS, bq), f32)],
        compiler_params=_params("parallel", "parallel", "arbitrary"),
        name="fox_attn",
    )(qx, kx, vx, kx_meta, vx_meta, main)


def _gla_kernel(q_ref, k_ref, v_ref, r_ref, a_ref, w2_ref, ba_ref, og_ref, s0_ref,
                o_ref, sout_ref, st_ref, b_sc, k_sc, *, C):
    c = pl.program_id(1)

    @pl.when(c == 0)
    def _():
        st_ref[...] = s0_ref[...]

    g = _log_sigmoid(_dot(a_ref[...], w2_ref[...], _HI) + ba_ref[...]) * (1.0 / GLA_GATE_NORMALIZER)
    b_sc[...] = _dot(_tril(C).astype(f32), g, _HI)
    k_sc[...] = k_ref[...].astype(f32)

    nb = C // SUB_BLOCK
    trow = lax.broadcasted_iota(jnp.int32, (SUB_BLOCK, 1), 0)
    lane = lax.broadcasted_iota(jnp.int32, (SUB_BLOCK, SUB_BLOCK), 1)

    for h in range(GLA_HEADS):
        sl = slice(h * GLA_DK, (h + 1) * GLA_DK)
        vs = slice(h * GLA_DV, (h + 1) * GLA_DV)
        bh = b_sc[:, sl]
        qh = q_ref[:, sl].astype(f32) * (GLA_DK ** -0.5)
        kh = k_sc[:, sl]
        vh = v_ref[:, vs]
        blast = bh[C - 1:C, :]
        st = st_ref[h]
        o_inter = _dot_nt((qh * jnp.exp(bh)).astype(bf16), st.astype(bf16))

        rows = []
        for i in range(nb):
            r0 = i * SUB_BLOCK
            bi = bh[r0:r0 + SUB_BLOCK]
            qi = qh[r0:r0 + SUB_BLOCK]

            ad = jnp.zeros((SUB_BLOCK, SUB_BLOCK), f32)
            for s in range(SUB_BLOCK):
                bs = b_sc[r0 + s:r0 + s + 1, sl]
                ks = k_sc[r0 + s:r0 + s + 1, sl]
                e = jnp.exp(jnp.where(trow >= s, bi - bs, -jnp.inf))
                col = jnp.sum(qi * ks * e, axis=-1, keepdims=True)
                ad = jnp.where(lane == s, col, ad)
            o_i = _dot(ad.astype(bf16), vh[r0:r0 + SUB_BLOCK])
            if i > 0:
                ref = bi[0:1, :]
                q_rel = (qi * jnp.exp(bi - ref)).astype(bf16)
                k_rel = (kh[:r0] * jnp.exp(ref - bh[:r0])).astype(bf16)
                o_i = o_i + _dot(_dot_nt(q_rel, k_rel).astype(bf16), vh[:r0])
            rows.append(o_i)
        o = jnp.concatenate(rows, axis=0) + o_inter if nb > 1 else rows[0] + o_inter

        k_dec = (kh * jnp.exp(blast - bh)).astype(bf16)
        st_ref[h] = st * jnp.exp(blast) + _dot_tn(vh, k_dec)

        r = r_ref[:, vs].astype(f32)
        o_ref[:, vs] = (_rmsnorm(o, og_ref[...]) * (r * _sigmoid(r))).astype(bf16)

    @pl.when(c == pl.num_programs(1) - 1)
    def _():
        sout_ref[...] = st_ref[...]


def _gla_mix(main, small, w_alpha2, b_alpha, o_gain, s0, C):
    nb, s, _ = main.shape
    blk = lambda w, col: pl.BlockSpec((None, C, w), lambda b, c: (b, c, col))
    state = (GLA_HEADS, GLA_DV, GLA_DK)
    return pl.pallas_call(
        functools.partial(_gla_kernel, C=C),
        grid=(nb, s // C),
        in_specs=[blk(GLA_QK, 0), blk(GLA_QK, 1), blk(GLA_V, 1), blk(GLA_V, 2),
                  pl.BlockSpec((None, C, LANES), lambda b, c: (b, c, 0)),
                  _resident((LANES, GLA_QK)), _resident((1, GLA_QK)), _resident((1, GLA_DV)),
                  pl.BlockSpec((None,) + state, lambda b, c: (0, 0, 0, 0))],
        out_specs=[blk(GLA_V, 0), pl.BlockSpec((None,) + state, lambda b, c: (b, 0, 0, 0))],
        out_shape=[jax.ShapeDtypeStruct((nb, s, GLA_V), bf16),
                   jax.ShapeDtypeStruct((nb,) + state, f32)],
        scratch_shapes=[pltpu.VMEM(state, f32), pltpu.VMEM((C, GLA_QK), f32),
                        pltpu.VMEM((C, GLA_QK), f32)],
        compiler_params=_params("parallel", "arbitrary"),
        name="gla_mix",
    )(main, main, main, main, small, w_alpha2, b_alpha, o_gain, s0)


def _unit_lower_inverse(lower, n, span):
    r = lax.broadcasted_iota(jnp.int32, (n, n), 0)
    c = lax.broadcasted_iota(jnp.int32, (n, n), 1)
    eye = (r == c).astype(f32)
    same_block = (r // SUB_BLOCK) == (c // SUB_BLOCK)
    m = jnp.where(same_block, -lower, 0.0)
    p = eye + m
    for _ in range(3):
        m = _bdot(m, m)
        p = p + _bdot(p, m)
    if span == SUB_BLOCK:
        return p
    assert span // SUB_BLOCK <= 4
    e = _bdot(p, jnp.where(same_block, 0.0, lower))
    e2 = _bdot(e, e)
    x = eye - e + e2 - _bdot(e, e2)
    return _bdot(x, p)


def _gdn_kernel(x_ref, gate_ref, sm_ref, cw_ref, alog_ref, dtb_ref, og_ref, tail0_ref, s0_ref,
                o_ref, sout_ref, xbuf, s_ref, *, C):
    c = pl.program_id(1)

    @pl.when(c == 0)
    def _():
        xbuf[0:8, :] = tail0_ref[...]
        s_ref[...] = s0_ref[...]

    xbuf[8:8 + C, :] = x_ref[...].astype(f32)

    def conv_silu(col):
        sl = slice(col, col + LANES)
        y = cw_ref[0:1, sl] * xbuf[5:5 + C, sl]
        for j in range(1, GDN_CONV):
            y = y + cw_ref[j:j + 1, sl] * xbuf[5 + j:5 + j + C, sl]
        return y * _sigmoid(y)

    def l2norm(x):
        return x * lax.rsqrt(jnp.sum(x * x, axis=-1, keepdims=True) + NORM_EPS)

    sm = sm_ref[...]
    g_all = -jnp.exp(alog_ref[...]) * _softplus(sm + dtb_ref[...])
    beta_all = _sigmoid(sm)
    b_all = _dot(_tril(C).astype(f32), g_all, _HI)
    hd = GDN_HEADS * GDN_DK
    n = GDN_GROUP * C

    rr = lax.broadcasted_iota(jnp.int32, (n, n), 0)
    cc = lax.broadcasted_iota(jnp.int32, (n, n), 1)
    strict = ((rr // C) == (cc // C)) & (rr > cc)
    eye = (rr == cc).astype(f32)

    for grp in range(GDN_HEADS // GDN_GROUP):
        heads = range(grp * GDN_GROUP, (grp + 1) * GDN_GROUP)
        stack = lambda f: jnp.concatenate([f(h) for h in heads], axis=0)
        q_st = stack(lambda h: l2norm(conv_silu(h * GDN_DK))) * (GDN_DK ** -0.5)
        k_st = stack(lambda h: l2norm(conv_silu(hd + h * GDN_DK)))
        v_st = stack(lambda h: conv_silu(2 * hd + h * GDN_DV))
        b_st = stack(lambda h: b_all[:, h:h + 1])
        beta_st = stack(lambda h: beta_all[:, GDN_HEADS + h:GDN_HEADS + h + 1])
        b_row = jnp.broadcast_to(b_st, (n, LANES)).T[0:1, :]

        dec = jnp.exp(jnp.where(strict, b_st - b_row, -jnp.inf))
        kb_st = k_st * beta_st
        k16 = k_st.astype(bf16)
        lower = _dot_nt(kb_st.astype(bf16), k16) * dec
        t_inv = _unit_lower_inverse(lower, n, C)
        e_b = jnp.exp(b_st)
        sol = _bdot(t_inv, jnp.concatenate([v_st * beta_st, kb_st * e_b], axis=1))
        u_st, w_st = sol[:, :GDN_DV], sol[:, GDN_DV:]
        attn = (_dot_nt(q_st.astype(bf16), k16) * (dec + eye)).astype(bf16)
        qe_st = (q_st * e_b).astype(bf16)

        states = [s_ref[h] for h in heads]
        s16 = [st.astype(bf16) for st in states]
        rows = [slice(i * C, (i + 1) * C) for i in range(GDN_GROUP)]
        v_new = jnp.concatenate([u_st[rows[i]] - _dot(w_st[rows[i]].astype(bf16), s16[i])
                                 for i in range(GDN_GROUP)], axis=0)
        vn16 = v_new.astype(bf16)
        o_st = _dot(attn, vn16) + jnp.concatenate(
            [_dot(qe_st[rows[i]], s16[i]) for i in range(GDN_GROUP)], axis=0)

        for i, h in enumerate(heads):
            b_h = b_st[rows[i]]
            b_last = b_h[C - 1:C, :]
            k_dec = (k_st[rows[i]] * jnp.exp(b_last - b_h)).astype(bf16)
            s_ref[h] = states[i] * jnp.exp(b_last) + _dot_tn(k_dec, vn16[rows[i]])
            vs = slice(h * GDN_DV, (h + 1) * GDN_DV)
            gt = gate_ref[:, vs].astype(f32)
            o_ref[:, vs] = (_rmsnorm(o_st[rows[i]], og_ref[...]) * (gt * _sigmoid(gt))).astype(bf16)

    xbuf[0:8, :] = xbuf[C:C + 8, :]

    @pl.when(c == pl.num_programs(1) - 1)
    def _():
        sout_ref[...] = s_ref[...]


def _gdn_mix(main, small, conv_w, a_log, dt_bias, o_gain, tail0, s0, C):
    nb, s, _ = main.shape
    state = (GDN_HEADS, GDN_DK, GDN_DV)
    return pl.pallas_call(
        functools.partial(_gdn_kernel, C=C),
        grid=(nb, s // C),
        in_specs=[pl.BlockSpec((None, C, GDN_QKV), lambda b, c: (b, c, 0)),
                  pl.BlockSpec((None, C, D_MODEL), lambda b, c: (b, c, GDN_QKV // D_MODEL)),
                  pl.BlockSpec((None, C, LANES), lambda b, c: (b, c, 0)),
                  _resident((GDN_CONV, GDN_QKV)), _resident((1, LANES)), _resident((1, LANES)),
                  _resident((1, GDN_DV)), _resident((8, GDN_QKV)),
                  pl.BlockSpec((None,) + state, lambda b, c: (0, 0, 0, 0))],
        out_specs=[pl.BlockSpec((None, C, D_MODEL), lambda b, c: (b, c, 0)),
                   pl.BlockSpec((None,) + state, lambda b, c: (b, 0, 0, 0))],
        out_shape=[jax.ShapeDtypeStruct((nb, s, D_MODEL), bf16),
                   jax.ShapeDtypeStruct((nb,) + state, f32)],
        scratch_shapes=[pltpu.VMEM((8 + C, GDN_QKV), f32), pltpu.VMEM(state, f32)],
        compiler_params=_params("parallel", "arbitrary"),
        name="gdn_mix",
    )(main, main, small, conv_w, a_log, dt_bias, o_gain, tail0, s0)


def _split_in_proj(w_in, n_main):
    w_small = jnp.pad(w_in[:, n_main:], ((0, 0), (0, LANES - (w_in.shape[1] - n_main))))
    return w_in[:, :n_main].astype(bf16), w_small.astype(bf16)


def _lane_row(v, width=LANES, offset=0):
    return jnp.pad(v.astype(f32), (offset, width - offset - v.shape[0])).reshape(1, width)


def kernel(x, meta_tokens, norm_mix, norm_ffn, w_gate_up, w_down, fox_w_in, fox_b_f, fox_q_gain, fox_k_gain, fox_w_out, gla_w_in, gla_w_alpha2, gla_b_alpha, gla_o_gain, gla_w_out, gdn_w_in, gdn_conv_w, gdn_a_log, gdn_dt_bias, gdn_o_gain, gdn_w_out):
    nb, seq, _ = x.shape
    depth = norm_mix.shape[0]
    hr = x.reshape(nb * seq, D_MODEL)
    hm = meta_tokens.astype(x.dtype)
    fox_tables = _fox_bias_tables()

    for i in range(depth):
        kind, j = i % 3, i // 3
        if kind == 0:
            w_main, w_small = _split_in_proj(fox_w_in[j], 4 * D_MODEL)
            w_out = fox_w_out[j]
            main_m, small_m = _norm_proj(hm, norm_mix[i], w_main, w_small)
            main_r, small_r = _norm_proj(hr, norm_mix[i], w_main, w_small)
            main_m = main_m.reshape(1, N_META, -1)
            main_r = main_r.reshape(nb, seq, -1)
            b_f = _lane_row(fox_b_f[j])
            qg = jnp.tile(fox_q_gain[j], 2).reshape(1, LANES) * (FOX_HEAD_DIM ** -0.5 * LOG2E)
            kg = jnp.tile(fox_k_gain[j], 2).reshape(1, LANES)
            qx_m, kx_m, vx_m, c_m = _fox_prep(main_m, small_m.reshape(1, N_META, LANES), b_f, qg, kg,
                                              jnp.zeros((1, LANES), f32), fox_tables, N_META)
            qx_r, kx_r, vx_r, _ = _fox_prep(main_r, small_r.reshape(nb, seq, LANES), b_f, qg, kg,
                                            c_m.reshape(1, LANES), fox_tables, min(FOX_BLOCK, seq))
            o_m = _fox_attn(qx_m, kx_m, vx_m, kx_m, vx_m, main_m, True)
            o_r = _fox_attn(qx_r, kx_r, vx_r, kx_m, vx_m, main_r, False)
        elif kind == 1:
            w_main, w_small = _split_in_proj(gla_w_in[j], 2 * GLA_QK + 2 * GLA_V)
            w_out = gla_w_out[j]
            main_m, small_m = _norm_proj(hm, norm_mix[i], w_main, w_small)
            main_r, small_r = _norm_proj(hr, norm_mix[i], w_main, w_small)
            w2 = jnp.pad(gla_w_alpha2[j], ((0, LANES - gla_w_alpha2.shape[1]), (0, 0)))
            ba = gla_b_alpha[j].reshape(1, GLA_QK)
            og = gla_o_gain[j].reshape(1, GLA_DV)
            s0 = jnp.zeros((1, GLA_HEADS, GLA_DV, GLA_DK), f32)
            o_m, s_m = _gla_mix(main_m.reshape(1, N_META, -1), small_m.reshape(1, N_META, LANES),
                                w2, ba, og, s0, N_META)
            o_r, _ = _gla_mix(main_r.reshape(nb, seq, -1), small_r.reshape(nb, seq, LANES),
                              w2, ba, og, s_m, GLA_CHUNK)
        else:
            w_main, w_small = _split_in_proj(gdn_w_in[j], GDN_QKV + GDN_HEADS * GDN_DV)
            w_out = gdn_w_out[j]
            main_m, small_m = _norm_proj(hm, norm_mix[i], w_main, w_small)
            main_r, small_r = _norm_proj(hr, norm_mix[i], w_main, w_small)
            cw = gdn_conv_w[j].reshape(GDN_CONV, GDN_QKV)
            alog = _lane_row(gdn_a_log[j])
            dtb = _lane_row(gdn_dt_bias[j])
            og = gdn_o_gain[j].reshape(1, GDN_DV)
            s0 = jnp.zeros((1, GDN_HEADS, GDN_DK, GDN_DV), f32)
            o_m, s_m = _gdn_mix(main_m.reshape(1, N_META, -1), small_m.reshape(1, N_META, LANES),
                                cw, alog, dtb, og, jnp.zeros((8, GDN_QKV), f32), s0, N_META)
            tail = main_m[N_META - 8:, :GDN_QKV].astype(f32)
            o_r, _ = _gdn_mix(main_r.reshape(nb, seq, -1), small_r.reshape(nb, seq, LANES),
                              cw, alog, dtb, og, tail, s_m, GDN_CHUNK)

        wo = w_out.astype(bf16)
        wgu = w_gate_up[i].astype(bf16)
        wd = w_down[i].astype(bf16)
        hm = _mix_ffn(hm, o_m.reshape(N_META, D_MODEL), wo, norm_ffn[i], wgu, wd)
        hr = _mix_ffn(hr, o_r.reshape(nb * seq, D_MODEL), wo, norm_ffn[i], wgu, wd)

    return hr.reshape(nb, seq, D_MODEL)
```

```python
import functools
import math

import jax
import jax.numpy as jnp
from jax import lax
from jax.experimental import pallas as pl
from jax.experimental.pallas import tpu as pltpu

f32 = jnp.float32
bf16 = jnp.bfloat16

D_MODEL = 1024
N_META = 16
NORM_EPS = 1e-6
LANES = 128
SUB_BLOCK = 16
LOG2E = math.log2(math.e)

FOX_HEADS = 16
FOX_HEAD_DIM = 64
FOX_BLOCK = 256
FOX_STEP_HEADS = 8
FOX_V_ROWS = FOX_HEAD_DIM + 16
FOX_BIAS_LANE = FOX_HEAD_DIM

GLA_HEADS = 4
GLA_DK = 128
GLA_DV = 256
GLA_QK = GLA_HEADS * GLA_DK
GLA_V = GLA_HEADS * GLA_DV
GLA_GATE_NORMALIZER = 16.0
GLA_CHUNK = 64

GDN_HEADS = 8
GDN_DK = 128
GDN_DV = 128
GDN_CONV = 4
GDN_CHUNK = 64
GDN_QKV = 3 * GDN_HEADS * GDN_DK
GDN_GROUP = 4
GDN_STEP_CHUNKS = 2

D_FF = 2816
FFN_CHUNK = 256

ROW_TILE = 512
VMEM_LIMIT = 56 * 2 ** 20

_HI = lax.Precision.HIGHEST


def _params(*sem):
    return pltpu.CompilerParams(dimension_semantics=sem, vmem_limit_bytes=VMEM_LIMIT)


def _resident(shape):
    nd = len(shape)
    return pl.BlockSpec(shape, lambda *_: (0,) * nd, pipeline_mode=pl.Buffered(1))


def _dot(a, b, precision=None):
    return jnp.dot(a, b, preferred_element_type=f32, precision=precision)


def _dot_nt(a, b, precision=None):
    return lax.dot_general(a, b, (((1,), (1,)), ((), ())), preferred_element_type=f32,
                           precision=precision)


def _dot_tn(a, b, precision=None):
    return lax.dot_general(a, b, (((0,), (0,)), ((), ())), preferred_element_type=f32,
                           precision=precision)


def _bdot(a, b):
    return _dot(a.astype(bf16), b.astype(bf16))


def _sigmoid(x):
    return 1.0 / (1.0 + jnp.exp(-x))


def _softplus(x):
    return jnp.maximum(x, 0.0) + jnp.log(1.0 + jnp.exp(-jnp.abs(x)))


def _log_sigmoid(x):
    return -_softplus(-x)


def _rmsnorm(x, g):
    return x * lax.rsqrt(jnp.mean(x * x, axis=-1, keepdims=True) + NORM_EPS) * g


def _tril(n, k=0):
    r = lax.broadcasted_iota(jnp.int32, (n, n), 0)
    c = lax.broadcasted_iota(jnp.int32, (n, n), 1)
    return r + k >= c


def _norm_proj_kernel(h_ref, g_ref, w_ref, ws_ref, o_ref, os_ref, *, tn):
    y = _rmsnorm(h_ref[...], g_ref[...]).astype(bf16)
    os_ref[...] = _dot(y, ws_ref[...])
    for j in range(w_ref.shape[1] // tn):
        o_ref[:, j * tn:(j + 1) * tn] = _dot(y, w_ref[:, j * tn:(j + 1) * tn]).astype(bf16)


def _norm_proj(h, gain, w_main, w_small):
    m = h.shape[0]
    tm = min(ROW_TILE, m)
    n = w_main.shape[1]
    return pl.pallas_call(
        functools.partial(_norm_proj_kernel, tn=512),
        grid=(m // tm,),
        in_specs=[pl.BlockSpec((tm, D_MODEL), lambda i: (i, 0)),
                  _resident((1, D_MODEL)), _resident(w_main.shape), _resident(w_small.shape)],
        out_specs=[pl.BlockSpec((tm, n), lambda i: (i, 0)),
                   pl.BlockSpec((tm, LANES), lambda i: (i, 0))],
        out_shape=[jax.ShapeDtypeStruct((m, n), bf16), jax.ShapeDtypeStruct((m, LANES), f32)],
        compiler_params=_params("parallel"),
        name="norm_proj",
    )(h, gain.reshape(1, D_MODEL), w_main, w_small)


def _mix_ffn_kernel(h_ref, o_ref, wo_ref, g_ref, wgu_ref, wd_ref, out_ref):
    hmid = h_ref[...] + _dot(o_ref[...], wo_ref[...])
    y = _rmsnorm(hmid, g_ref[...]).astype(bf16)
    out_ref[...] = hmid
    for f in range(D_FF // FFN_CHUNK):
        lo = f * FFN_CHUNK
        gt = _dot(y, wgu_ref[:, lo:lo + FFN_CHUNK])
        up = _dot(y, wgu_ref[:, D_FF + lo:D_FF + lo + FFN_CHUNK])
        act = (gt * _sigmoid(gt) * up).astype(bf16)
        out_ref[...] += _dot(act, wd_ref[lo:lo + FFN_CHUNK, :])


def _mix_ffn(h, o, w_out, gain, w_gate_up, w_down):
    m = h.shape[0]
    tm = min(ROW_TILE, m)
    row = lambda i: (i, 0)
    return pl.pallas_call(
        _mix_ffn_kernel,
        grid=(m // tm,),
        in_specs=[pl.BlockSpec((tm, D_MODEL), row), pl.BlockSpec((tm, D_MODEL), row),
                  _resident(w_out.shape), _resident((1, D_MODEL)),
                  _resident(w_gate_up.shape), _resident(w_down.shape)],
        out_specs=pl.BlockSpec((tm, D_MODEL), row),
        out_shape=jax.ShapeDtypeStruct((m, D_MODEL), f32),
        compiler_params=_params("parallel"),
        name="mix_ffn",
    )(h, o, w_out, gain.reshape(1, D_MODEL), w_gate_up, w_down)


def _fox_bias_tables():
    r = jnp.arange(3 * LANES)[:, None]
    c = jnp.arange(FOX_HEADS * LANES)[None, :]
    head, lane = c // LANES, c % LANES
    piece, src_head = r // LANES, r % LANES
    hit = src_head == head
    place_q = (hit & (lane == FOX_BIAS_LANE + 3 + piece)).astype(bf16)
    place_k = (hit & (lane == FOX_BIAS_LANE + piece)).astype(bf16)
    lane1 = jnp.arange(FOX_HEADS * LANES)[None, :] % LANES
    const_q = -((lane1 >= FOX_BIAS_LANE) & (lane1 < FOX_BIAS_LANE + 3)).astype(f32)
    const_k = ((lane1 >= FOX_BIAS_LANE + 3) & (lane1 < FOX_BIAS_LANE + 6)).astype(f32)
    return place_q, place_k, const_q, const_k


def _fox_prep_kernel(q_ref, k_ref, v_ref, f_ref, bf_ref, qg_ref, kg_ref, c0_ref,
                     pq_ref, pk_ref, cq_ref, ck_ref,
                     qx_ref, kx_ref, vx_ref, cend_ref, carry_ref, *, tp):
    @pl.when(pl.program_id(1) == 0)
    def _():
        carry_ref[...] = c0_ref[...]

    log_f = _log_sigmoid(f_ref[...] + bf_ref[...])
    cum = _dot(_tril(tp).astype(f32), log_f, _HI) + carry_ref[...]
    carry_ref[...] = cum[tp - 1:tp, :]
    cend_ref[...] = cum[tp - 1:tp, :]
    c2 = cum * LOG2E
    hi = c2.astype(bf16)
    r1 = c2 - hi.astype(f32)
    mid = r1.astype(bf16)
    low = (r1 - mid.astype(f32)).astype(bf16)
    x3 = jnp.concatenate([hi, mid, low], axis=1)
    ext_q = _dot(x3, pq_ref[...]) + cq_ref[...]
    ext_k = _dot(x3, pk_ref[...]) + ck_ref[...]

    lo = lax.broadcasted_iota(jnp.int32, (tp, LANES), 1) < FOX_HEAD_DIM

    def head_norm(x, g):
        x2 = x * x
        s_lo = jnp.sum(jnp.where(lo, x2, 0.0), axis=-1, keepdims=True)
        s_hi = jnp.sum(jnp.where(lo, 0.0, x2), axis=-1, keepdims=True)
        ms = jnp.where(lo, s_lo, s_hi) * (1.0 / FOX_HEAD_DIM)
        return x * lax.rsqrt(ms + NORM_EPS) * g

    ones = jnp.ones((FOX_V_ROWS - FOX_HEAD_DIM, tp), bf16)
    for p in range(FOX_HEADS // 2):
        sl = slice(p * LANES, (p + 1) * LANES)
        qn = head_norm(q_ref[:, sl].astype(f32), qg_ref[...])
        kn = head_norm(k_ref[:, sl].astype(f32), kg_ref[...])
        vt = v_ref[:, sl].astype(f32).T
        halves = ((qn, kn), (pltpu.roll(qn, FOX_HEAD_DIM, axis=1), pltpu.roll(kn, FOX_HEAD_DIM, axis=1)))
        for hh, (qv, kv) in enumerate(halves):
            h = 2 * p + hh
            hs = slice(h * LANES, (h + 1) * LANES)
            qx_ref[h] = jnp.where(lo, qv, ext_q[:, hs]).astype(bf16)
            kx_ref[h] = jnp.where(lo, kv, ext_k[:, hs]).astype(bf16)
            vx_ref[h, 0:FOX_HEAD_DIM, :] = vt[hh * FOX_HEAD_DIM:(hh + 1) * FOX_HEAD_DIM].astype(bf16)
            vx_ref[h, FOX_HEAD_DIM:FOX_V_ROWS, :] = ones


def _fox_prep(main, small, b_f, q_gain, k_gain, c0, tables, tp):
    nb, s, _ = main.shape
    blk = lambda col: pl.BlockSpec((None, tp, D_MODEL), lambda b, i: (b, i, col))
    head_blk = pl.BlockSpec((None, FOX_HEADS, tp, LANES), lambda b, i: (b, 0, i, 0))
    return pl.pallas_call(
        functools.partial(_fox_prep_kernel, tp=tp),
        grid=(nb, s // tp),
        in_specs=[blk(0), blk(1), blk(2), pl.BlockSpec((None, tp, LANES), lambda b, i: (b, i, 0)),
                  _resident((1, LANES)), _resident((1, LANES)), _resident((1, LANES)),
                  _resident((1, LANES))] + [_resident(t.shape) for t in tables],
        out_specs=[head_blk, head_blk,
                   pl.BlockSpec((None, FOX_HEADS, None, FOX_V_ROWS, tp), lambda b, i: (b, 0, i, 0, 0)),
                   pl.BlockSpec((None, 1, LANES), lambda b, i: (b, 0, 0))],
        out_shape=[jax.ShapeDtypeStruct((nb, FOX_HEADS, s, LANES), bf16),
                   jax.ShapeDtypeStruct((nb, FOX_HEADS, s, LANES), bf16),
                   jax.ShapeDtypeStruct((nb, FOX_HEADS, s // tp, FOX_V_ROWS, tp), bf16),
                   jax.ShapeDtypeStruct((nb, 1, LANES), f32)],
        scratch_shapes=[pltpu.VMEM((1, LANES), f32)],
        compiler_params=_params("parallel", "arbitrary"),
        name="fox_prep",
    )(main, main, main, small, b_f, q_gain, k_gain, c0, *tables)


def _fox_attn_kernel(qx_ref, kx_ref, vx_ref, kxm_ref, vxm_ref, gate_ref, o_ref, m_ref, acc_ref,
                     s_ref, mx_ref, *, bq, nh, meta_only):
    qi = pl.program_id(2)
    heads = range(nh)
    visible = (lax.broadcasted_iota(jnp.int32, (bq, bq), 0)
               <= lax.broadcasted_iota(jnp.int32, (bq, bq), 1))

    def scores(j, diagonal):
        start = pl.multiple_of(j * bq, bq)
        ss = [_dot_nt(kx_ref[h, pl.ds(start, bq), :], qx_ref[h]) for h in heads]
        if diagonal:
            ss = [jnp.where(visible, s, -jnp.inf) for s in ss]
        return tuple(ss), tuple(jnp.max(s, axis=0, keepdims=True) for s in ss)

    def stage(blk):
        ss, mx = blk
        for h in heads:
            s_ref[h] = ss[h]
            mx_ref[h] = mx[h]

    if not meta_only:
        stage(scores(qi, True))

    sm = [_dot_nt(kxm_ref[h], qx_ref[h]) for h in heads]
    if meta_only:
        sm = [jnp.where(visible, s, -jnp.inf) for s in sm]
    ms = [jnp.max(s, axis=0, keepdims=True) for s in sm]
    ps = [jnp.exp2(s - m).astype(bf16) for s, m in zip(sm, ms)]
    for h in heads:
        m_ref[h] = ms[h]
        acc_ref[h] = _dot(vxm_ref[h], ps[h])

    if not meta_only:
        def consume(j):
            m_old = [m_ref[h] for h in heads]
            m_new = [jnp.maximum(m_old[h], mx_ref[h]) for h in heads]
            p = [jnp.exp2(s_ref[h] - m_new[h]).astype(bf16) for h in heads]
            pv = [_dot(vx_ref[h, j], p[h]) for h in heads]
            for h in heads:
                acc_ref[h] = jnp.exp2(m_old[h] - m_new[h]) * acc_ref[h] + pv[h]
                m_ref[h] = m_new[h]

        def body(j, j_prev):
            nxt = scores(j, False)
            consume(j_prev)
            stage(nxt)
            return j

        consume(lax.fori_loop(0, qi, body, qi))

    for pair in range(nh // 2):
        outs = []
        for h in (2 * pair, 2 * pair + 1):
            a = acc_ref[h]
            outs.append(a[0:FOX_HEAD_DIM] / a[FOX_HEAD_DIM:FOX_HEAD_DIM + 1])
        o = jnp.concatenate(outs, axis=0).T
        cols = slice(pair * LANES, (pair + 1) * LANES)
        o_ref[:, cols] = (o * _sigmoid(gate_ref[:, cols].astype(f32))).astype(bf16)


def _fox_attn(qx, kx, vx, kx_meta, vx_meta, main, meta_only):
    nb, _, s, _ = qx.shape
    bq = min(FOX_BLOCK, s)
    nh = FOX_STEP_HEADS
    w = nh // 2 * LANES
    gcol = 3 * D_MODEL // w
    return pl.pallas_call(
        functools.partial(_fox_attn_kernel, bq=bq, nh=nh, meta_only=meta_only),
        grid=(nb, FOX_HEADS // nh, s // bq),
        in_specs=[pl.BlockSpec((None, nh, bq, LANES), lambda b, p, i: (b, p, i, 0)),
                  pl.BlockSpec((None, nh, s, LANES), lambda b, p, i: (b, p, 0, 0)),
                  pl.BlockSpec((None, nh, s // bq, FOX_V_ROWS, bq), lambda b, p, i: (b, p, 0, 0, 0)),
                  pl.BlockSpec((None, nh, N_META, LANES), lambda b, p, i: (0, p, 0, 0)),
                  pl.BlockSpec((None, nh, None, FOX_V_ROWS, N_META), lambda b, p, i: (0, p, 0, 0, 0)),
                  pl.BlockSpec((None, bq, w), lambda b, p, i: (b, i, gcol + p))],
        out_specs=pl.BlockSpec((None, bq, w), lambda b, p, i: (b, i, p)),
        out_shape=jax.ShapeDtypeStruct((nb, s, D_MODEL), bf16),
        scratch_shapes=[pltpu.VMEM((nh, 1, bq), f32), pltpu.VMEM((nh, FOX_V_ROWS, bq), f32),
                        pltpu.VMEM((nh, bq, bq), f32), pltpu.VMEM((nh, 1, bq), f32)],
        compiler_params=_params("parallel", "parallel", "arbitrary"),
        name="fox_attn",
    )(qx, kx, vx, kx_meta, vx_meta, main)


def _gla_kernel(q_ref, k_ref, v_ref, r_ref, a_ref, w2_ref, ba_ref, og_ref, s0_ref,
                o_ref, sout_ref, st_ref, b_sc, k_sc, *, C):
    c = pl.program_id(1)

    @pl.when(c == 0)
    def _():
        st_ref[...] = s0_ref[...]

    g = _log_sigmoid(_dot(a_ref[...], w2_ref[...], _HI) + ba_ref[...]) * (1.0 / GLA_GATE_NORMALIZER)
    b_sc[...] = _dot(_tril(C).astype(f32), g, _HI)
    k_sc[...] = k_ref[...].astype(f32)

    nb = C // SUB_BLOCK
    trow = lax.broadcasted_iota(jnp.int32, (SUB_BLOCK, 1), 0)
    lane = lax.broadcasted_iota(jnp.int32, (SUB_BLOCK, SUB_BLOCK), 1)

    for h in range(GLA_HEADS):
        sl = slice(h * GLA_DK, (h + 1) * GLA_DK)
        vs = slice(h * GLA_DV, (h + 1) * GLA_DV)
        bh = b_sc[:, sl]
        qh = q_ref[:, sl].astype(f32) * (GLA_DK ** -0.5)
        kh = k_sc[:, sl]
        vh = v_ref[:, vs]
        blast = bh[C - 1:C, :]
        st = st_ref[h]
        o_inter = _dot_nt((qh * jnp.exp(bh)).astype(bf16), st.astype(bf16))

        rows = []
        for i in range(nb):
            r0 = i * SUB_BLOCK
            bi = bh[r0:r0 + SUB_BLOCK]
            qi = qh[r0:r0 + SUB_BLOCK]

            ad = jnp.zeros((SUB_BLOCK, SUB_BLOCK), f32)
            for s in range(SUB_BLOCK):
                bs = b_sc[r0 + s:r0 + s + 1, sl]
                ks = k_sc[r0 + s:r0 + s + 1, sl]
                e = jnp.exp(jnp.where(trow >= s, bi - bs, -jnp.inf))
                col = jnp.sum(qi * ks * e, axis=-1, keepdims=True)
                ad = jnp.where(lane == s, col, ad)
            o_i = _dot(ad.astype(bf16), vh[r0:r0 + SUB_BLOCK])
            if i > 0:
                ref = bi[0:1, :]
                q_rel = (qi * jnp.exp(bi - ref)).astype(bf16)
                k_rel = (kh[:r0] * jnp.exp(ref - bh[:r0])).astype(bf16)
                o_i = o_i + _dot(_dot_nt(q_rel, k_rel).astype(bf16), vh[:r0])
            rows.append(o_i)
        o = jnp.concatenate(rows, axis=0) + o_inter if nb > 1 else rows[0] + o_inter

        k_dec = (kh * jnp.exp(blast - bh)).astype(bf16)
        st_ref[h] = st * jnp.exp(blast) + _dot_tn(vh, k_dec)

        r = r_ref[:, vs].astype(f32)
        o_ref[:, vs] = (_rmsnorm(o, og_ref[...]) * (r * _sigmoid(r))).astype(bf16)

    @pl.when(c == pl.num_programs(1) - 1)
    def _():
        sout_ref[...] = st_ref[...]


def _gla_mix(main, small, w_alpha2, b_alpha, o_gain, s0, C):
    nb, s, _ = main.shape
    blk = lambda w, col: pl.BlockSpec((None, C, w), lambda b, c: (b, c, col))
    state = (GLA_HEADS, GLA_DV, GLA_DK)
    return pl.pallas_call(
        functools.partial(_gla_kernel, C=C),
        grid=(nb, s // C),
        in_specs=[blk(GLA_QK, 0), blk(GLA_QK, 1), blk(GLA_V, 1), blk(GLA_V, 2),
                  pl.BlockSpec((None, C, LANES), lambda b, c: (b, c, 0)),
                  _resident((LANES, GLA_QK)), _resident((1, GLA_QK)), _resident((1, GLA_DV)),
                  pl.BlockSpec((None,) + state, lambda b, c: (0, 0, 0, 0))],
        out_specs=[blk(GLA_V, 0), pl.BlockSpec((None,) + state, lambda b, c: (b, 0, 0, 0))],
        out_shape=[jax.ShapeDtypeStruct((nb, s, GLA_V), bf16),
                   jax.ShapeDtypeStruct((nb,) + state, f32)],
        scratch_shapes=[pltpu.VMEM(state, f32), pltpu.VMEM((C, GLA_QK), f32),
                        pltpu.VMEM((C, GLA_QK), f32)],
        compiler_params=_params("parallel", "arbitrary"),
        name="gla_mix",
    )(main, main, main, main, small, w_alpha2, b_alpha, o_gain, s0)


def _unit_lower_inverses(lowers, n, span):
    r = lax.broadcasted_iota(jnp.int32, (n, n), 0)
    c = lax.broadcasted_iota(jnp.int32, (n, n), 1)
    eye = (r == c).astype(f32)
    same_block = (r // SUB_BLOCK) == (c // SUB_BLOCK)
    ms = [jnp.where(same_block, -lower, 0.0) for lower in lowers]
    ps = [eye + m for m in ms]
    for _ in range(3):
        ms = [_bdot(m, m) for m in ms]
        ps = [p + _bdot(p, m) for p, m in zip(ps, ms)]
    if span == SUB_BLOCK:
        return ps
    assert span // SUB_BLOCK <= 4
    es = [_bdot(p, jnp.where(same_block, 0.0, lower)) for p, lower in zip(ps, lowers)]
    e2s = [_bdot(e, e) for e in es]
    xs = [eye - e + e2 - _bdot(e, e2) for e, e2 in zip(es, e2s)]
    return [_bdot(x, p) for x, p in zip(xs, ps)]


def _gdn_kernel(x_ref, gate_ref, sm_ref, cw_ref, alog_ref, dtb_ref, og_ref, tail0_ref, s0_ref,
                o_ref, sout_ref, xbuf, s_ref, *, C, nc):
    c = pl.program_id(1)
    T = nc * C

    @pl.when(c == 0)
    def _():
        xbuf[0:8, :] = tail0_ref[...]
        s_ref[...] = s0_ref[...]

    xbuf[8:8 + T, :] = x_ref[...].astype(f32)

    def conv_silu(col):
        sl = slice(col, col + LANES)
        y = cw_ref[0:1, sl] * xbuf[5:5 + T, sl]
        for j in range(1, GDN_CONV):
            y = y + cw_ref[j:j + 1, sl] * xbuf[5 + j:5 + j + T, sl]
        return y * _sigmoid(y)

    def l2norm(x):
        return x * lax.rsqrt(jnp.sum(x * x, axis=-1, keepdims=True) + NORM_EPS)

    sm = sm_ref[...]
    g_all = -jnp.exp(alog_ref[...]) * _softplus(sm + dtb_ref[...])
    beta_all = _sigmoid(sm)
    tr = lax.broadcasted_iota(jnp.int32, (T, T), 0)
    tc = lax.broadcasted_iota(jnp.int32, (T, T), 1)
    in_chunk_tril = (((tr // C) == (tc // C)) & (tr >= tc)).astype(f32)
    b_all = _dot(in_chunk_tril, g_all, _HI)
    hd = GDN_HEADS * GDN_DK
    n = GDN_GROUP * C

    rr = lax.broadcasted_iota(jnp.int32, (n, n), 0)
    cc = lax.broadcasted_iota(jnp.int32, (n, n), 1)
    strict = ((rr // C) == (cc // C)) & (rr > cc)
    eye = (rr == cc).astype(f32)

    groups = [range(g * GDN_GROUP, (g + 1) * GDN_GROUP) for g in range(GDN_HEADS // GDN_GROUP)]
    units = [(ck, g) for ck in range(nc) for g in range(len(groups))]
    rows = [slice(i * C, (i + 1) * C) for i in range(GDN_GROUP)]

    def stack(per_head):
        return [jnp.concatenate([per_head[h][ck * C:(ck + 1) * C] for h in groups[g]], axis=0)
                for ck, g in units]

    every = range(GDN_HEADS)
    q_st = stack([l2norm(conv_silu(h * GDN_DK)) * (GDN_DK ** -0.5) for h in every])
    k_st = stack([l2norm(conv_silu(hd + h * GDN_DK)) for h in every])
    v_st = stack([conv_silu(2 * hd + h * GDN_DV) for h in every])
    b_st = stack([b_all[:, h:h + 1] for h in every])
    beta_st = stack([beta_all[:, GDN_HEADS + h:GDN_HEADS + h + 1] for h in every])
    b_row = [jnp.broadcast_to(b, (n, LANES)).T[0:1, :] for b in b_st]

    dec = [jnp.exp(jnp.where(strict, b - br, -jnp.inf)) for b, br in zip(b_st, b_row)]
    kb_st = [k * beta for k, beta in zip(k_st, beta_st)]
    k16 = [k.astype(bf16) for k in k_st]
    lower = [_dot_nt(kb.astype(bf16), k) * d for kb, k, d in zip(kb_st, k16, dec)]
    attn = [(_dot_nt(q.astype(bf16), k) * (d + eye)).astype(bf16) for q, k, d in zip(q_st, k16, dec)]
    t_inv = _unit_lower_inverses(lower, n, C)
    e_b = [jnp.exp(b) for b in b_st]
    sol = [_bdot(t, jnp.concatenate([v * beta, kb * e], axis=1))
           for t, v, beta, kb, e in zip(t_inv, v_st, beta_st, kb_st, e_b)]
    qe_st = [(q * e).astype(bf16) for q, e in zip(q_st, e_b)]

    states = [s_ref[h] for h in every]
    for u, (ck, g) in enumerate(units):
        heads = groups[g]
        s16 = [states[h].astype(bf16) for h in heads]
        v_new = jnp.concatenate([sol[u][rows[i], :GDN_DV] - _dot(sol[u][rows[i], GDN_DV:].astype(bf16), s16[i])
                                 for i in range(GDN_GROUP)], axis=0)
        vn16 = v_new.astype(bf16)
        o_u = _dot(attn[u], vn16) + jnp.concatenate(
            [_dot(qe_st[u][rows[i]], s16[i]) for i in range(GDN_GROUP)], axis=0)
        for i, h in enumerate(heads):
            b_h = b_st[u][rows[i]]
            b_last = b_h[C - 1:C, :]
            k_dec = (k_st[u][rows[i]] * jnp.exp(b_last - b_h)).astype(bf16)
            states[h] = states[h] * jnp.exp(b_last) + _dot_tn(k_dec, vn16[rows[i]])
            vs = slice(h * GDN_DV, (h + 1) * GDN_DV)
            tok = slice(ck * C, (ck + 1) * C)
            gt = gate_ref[tok, vs].astype(f32)
            o_ref[tok, vs] = (_rmsnorm(o_u[rows[i]], og_ref[...]) * (gt * _sigmoid(gt))).astype(bf16)
    for h in every:
        s_ref[h] = states[h]

    xbuf[0:8, :] = xbuf[T:T + 8, :]

    @pl.when(c == pl.num_programs(1) - 1)
    def _():
        sout_ref[...] = s_ref[...]


def _gdn_mix(main, small, conv_w, a_log, dt_bias, o_gain, tail0, s0, C):
    nb, s, _ = main.shape
    state = (GDN_HEADS, GDN_DK, GDN_DV)
    nc = min(GDN_STEP_CHUNKS, s // C)
    t = nc * C
    return pl.pallas_call(
        functools.partial(_gdn_kernel, C=C, nc=nc),
        grid=(nb, s // t),
        in_specs=[pl.BlockSpec((None, t, GDN_QKV), lambda b, c: (b, c, 0)),
                  pl.BlockSpec((None, t, D_MODEL), lambda b, c: (b, c, GDN_QKV // D_MODEL)),
                  pl.BlockSpec((None, t, LANES), lambda b, c: (b, c, 0)),
                  _resident((GDN_CONV, GDN_QKV)), _resident((1, LANES)), _resident((1, LANES)),
                  _resident((1, GDN_DV)), _resident((8, GDN_QKV)),
                  pl.BlockSpec((None,) + state, lambda b, c: (0, 0, 0, 0))],
        out_specs=[pl.BlockSpec((None, t, D_MODEL), lambda b, c: (b, c, 0)),
                   pl.BlockSpec((None,) + state, lambda b, c: (b, 0, 0, 0))],
        out_shape=[jax.ShapeDtypeStruct((nb, s, D_MODEL), bf16),
                   jax.ShapeDtypeStruct((nb,) + state, f32)],
        scratch_shapes=[pltpu.VMEM((8 + t, GDN_QKV), f32), pltpu.VMEM(state, f32)],
        compiler_params=_params("parallel", "arbitrary"),
        name="gdn_mix",
    )(main, main, small, conv_w, a_log, dt_bias, o_gain, tail0, s0)


def _split_in_proj(w_in, n_main):
    w_small = jnp.pad(w_in[:, n_main:], ((0, 0), (0, LANES - (w_in.shape[1] - n_main))))
    return w_in[:, :n_main].astype(bf16), w_small.astype(bf16)


def _lane_row(v, width=LANES, offset=0):
    return jnp.pad(v.astype(f32), (offset, width - offset - v.shape[0])).reshape(1, width)


def kernel(x, meta_tokens, norm_mix, norm_ffn, w_gate_up, w_down, fox_w_in, fox_b_f, fox_q_gain, fox_k_gain, fox_w_out, gla_w_in, gla_w_alpha2, gla_b_alpha, gla_o_gain, gla_w_out, gdn_w_in, gdn_conv_w, gdn_a_log, gdn_dt_bias, gdn_o_gain, gdn_w_out):
    nb, seq, _ = x.shape
    depth = norm_mix.shape[0]
    hr = x.reshape(nb * seq, D_MODEL)
    hm = meta_tokens.astype(x.dtype)
    fox_tables = _fox_bias_tables()

    for i in range(depth):
        kind, j = i % 3, i // 3
        if kind == 0:
            w_main, w_small = _split_in_proj(fox_w_in[j], 4 * D_MODEL)
            w_out = fox_w_out[j]
            main_m, small_m = _norm_proj(hm, norm_mix[i], w_main, w_small)
            main_r, small_r = _norm_proj(hr, norm_mix[i], w_main, w_small)
            main_m = main_m.reshape(1, N_META, -1)
            main_r = main_r.reshape(nb, seq, -1)
            b_f = _lane_row(fox_b_f[j])
            qg = jnp.tile(fox_q_gain[j], 2).reshape(1, LANES) * (FOX_HEAD_DIM ** -0.5 * LOG2E)
            kg = jnp.tile(fox_k_gain[j], 2).reshape(1, LANES)
            qx_m, kx_m, vx_m, c_m = _fox_prep(main_m, small_m.reshape(1, N_META, LANES), b_f, qg, kg,
                                              jnp.zeros((1, LANES), f32), fox_tables, N_META)
            qx_r, kx_r, vx_r, _ = _fox_prep(main_r, small_r.reshape(nb, seq, LANES), b_f, qg, kg,
                                            c_m.reshape(1, LANES), fox_tables, min(FOX_BLOCK, seq))
            o_m = _fox_attn(qx_m, kx_m, vx_m, kx_m, vx_m, main_m, True)
            o_r = _fox_attn(qx_r, kx_r, vx_r, kx_m, vx_m, main_r, False)
        elif kind == 1:
            w_main, w_small = _split_in_proj(gla_w_in[j], 2 * GLA_QK + 2 * GLA_V)
            w_out = gla_w_out[j]
            main_m, small_m = _norm_proj(hm, norm_mix[i], w_main, w_small)
            main_r, small_r = _norm_proj(hr, norm_mix[i], w_main, w_small)
            w2 = jnp.pad(gla_w_alpha2[j], ((0, LANES - gla_w_alpha2.shape[1]), (0, 0)))
            ba = gla_b_alpha[j].reshape(1, GLA_QK)
            og = gla_o_gain[j].reshape(1, GLA_DV)
            s0 = jnp.zeros((1, GLA_HEADS, GLA_DV, GLA_DK), f32)
            o_m, s_m = _gla_mix(main_m.reshape(1, N_META, -1), small_m.reshape(1, N_META, LANES),
                                w2, ba, og, s0, N_META)
            o_r, _ = _gla_mix(main_r.reshape(nb, seq, -1), small_r.reshape(nb, seq, LANES),
                              w2, ba, og, s_m, GLA_CHUNK)
        else:
            w_main, w_small = _split_in_proj(gdn_w_in[j], GDN_QKV + GDN_HEADS * GDN_DV)
            w_out = gdn_w_out[j]
            main_m, small_m = _norm_proj(hm, norm_mix[i], w_main, w_small)
            main_r, small_r = _norm_proj(hr, norm_mix[i], w_main, w_small)
            cw = gdn_conv_w[j].reshape(GDN_CONV, GDN_QKV)
            alog = _lane_row(gdn_a_log[j])
            dtb = _lane_row(gdn_dt_bias[j])
            og = gdn_o_gain[j].reshape(1, GDN_DV)
            s0 = jnp.zeros((1, GDN_HEADS, GDN_DK, GDN_DV), f32)
            o_m, s_m = _gdn_mix(main_m.reshape(1, N_META, -1), small_m.reshape(1, N_META, LANES),
                                cw, alog, dtb, og, jnp.zeros((8, GDN_QKV), f32), s0, N_META)
            tail = main_m[N_META - 8:, :GDN_QKV].astype(f32)
            o_r, _ = _gdn_mix(main_r.reshape(nb, seq, -1), small_r.reshape(nb, seq, LANES),
                              cw, alog, dtb, og, tail, s_m, GDN_CHUNK)

        wo = w_out.astype(bf16)
        wgu = w_gate_up[i].astype(bf16)
        wd = w_down[i].astype(bf16)
        hm = _mix_ffn(hm, o_m.reshape(N_META, D_MODEL), wo, norm_ffn[i], wgu, wd)
        hr = _mix_ffn(hr, o_r.reshape(nb * seq, D_MODEL), wo, norm_ffn[i], wgu, wd)

    return hr.reshape(nb, seq, D_MODEL)
```

```python
import functools
import math

import jax
import jax.numpy as jnp
from jax import lax
from jax.experimental import pallas as pl
from jax.experimental.pallas import tpu as pltpu

f32 = jnp.float32
bf16 = jnp.bfloat16

D_MODEL = 1024
N_META = 16
NORM_EPS = 1e-6
LANES = 128
SUB_BLOCK = 16
LOG2E = math.log2(math.e)

FOX_HEADS = 16
FOX_HEAD_DIM = 64
FOX_BLOCK = 256
FOX_STEP_HEADS = 8
FOX_ZERO_EXP = 160.0
FOX_V_ROWS = FOX_HEAD_DIM + 16
FOX_BIAS_LANE = FOX_HEAD_DIM

GLA_HEADS = 4
GLA_DK = 128
GLA_DV = 256
GLA_QK = GLA_HEADS * GLA_DK
GLA_V = GLA_HEADS * GLA_DV
GLA_GATE_NORMALIZER = 16.0
GLA_CHUNK = 64
GLA_STEP_CHUNKS = 2

GDN_HEADS = 8
GDN_DK = 128
GDN_DV = 128
GDN_CONV = 4
GDN_CHUNK = 64
GDN_QKV = 3 * GDN_HEADS * GDN_DK
GDN_GROUP = 2
GDN_STEP_CHUNKS = 4

D_FF = 2816
FFN_CHUNK = 256

ROW_TILE = 512
VMEM_LIMIT = 56 * 2 ** 20

_HI = lax.Precision.HIGHEST


def _params(*sem):
    return pltpu.CompilerParams(dimension_semantics=sem, vmem_limit_bytes=VMEM_LIMIT)


def _resident(shape):
    nd = len(shape)
    return pl.BlockSpec(shape, lambda *_: (0,) * nd, pipeline_mode=pl.Buffered(1))


def _dot(a, b, precision=None):
    return jnp.dot(a, b, preferred_element_type=f32, precision=precision)


def _dot_nt(a, b, precision=None):
    return lax.dot_general(a, b, (((1,), (1,)), ((), ())), preferred_element_type=f32,
                           precision=precision)


def _dot_tn(a, b, precision=None):
    return lax.dot_general(a, b, (((0,), (0,)), ((), ())), preferred_element_type=f32,
                           precision=precision)


def _bdot(a, b):
    return _dot(a.astype(bf16), b.astype(bf16))


def _sigmoid(x):
    return 1.0 / (1.0 + jnp.exp(-x))


def _softplus(x):
    return jnp.maximum(x, 0.0) + jnp.log(1.0 + jnp.exp(-jnp.abs(x)))


def _log_sigmoid(x):
    return -_softplus(-x)


def _rmsnorm(x, g):
    return x * lax.rsqrt(jnp.mean(x * x, axis=-1, keepdims=True) + NORM_EPS) * g


def _tril(n, k=0):
    r = lax.broadcasted_iota(jnp.int32, (n, n), 0)
    c = lax.broadcasted_iota(jnp.int32, (n, n), 1)
    return r + k >= c


def _chunk_cumsum(x, chunk):
    pos = lax.broadcasted_iota(jnp.int32, (x.shape[0], 1), 0) % chunk
    d = 1
    while d < chunk:
        x = x + jnp.where(pos >= d, pltpu.roll(x, d, axis=0), 0.0)
        d *= 2
    return x


def _norm_proj_kernel(h_ref, g_ref, w_ref, ws_ref, o_ref, os_ref, *, tn):
    y = _rmsnorm(h_ref[...], g_ref[...]).astype(bf16)
    os_ref[...] = _dot(y, ws_ref[...])
    for j in range(w_ref.shape[1] // tn):
        o_ref[:, j * tn:(j + 1) * tn] = _dot(y, w_ref[:, j * tn:(j + 1) * tn]).astype(bf16)


def _norm_proj(h, gain, w_main, w_small):
    m = h.shape[0]
    tm = min(ROW_TILE, m)
    n = w_main.shape[1]
    return pl.pallas_call(
        functools.partial(_norm_proj_kernel, tn=512),
        grid=(m // tm,),
        in_specs=[pl.BlockSpec((tm, D_MODEL), lambda i: (i, 0)),
                  _resident((1, D_MODEL)), _resident(w_main.shape), _resident(w_small.shape)],
        out_specs=[pl.BlockSpec((tm, n), lambda i: (i, 0)),
                   pl.BlockSpec((tm, LANES), lambda i: (i, 0))],
        out_shape=[jax.ShapeDtypeStruct((m, n), bf16), jax.ShapeDtypeStruct((m, LANES), f32)],
        compiler_params=_params("parallel"),
        name="norm_proj",
    )(h, gain.reshape(1, D_MODEL), w_main, w_small)


def _mix_ffn_kernel(h_ref, o_ref, wo_ref, g_ref, wgu_ref, wd_ref, out_ref):
    hmid = h_ref[...] + _dot(o_ref[...], wo_ref[...])
    y = _rmsnorm(hmid, g_ref[...]).astype(bf16)
    out_ref[...] = hmid
    for f in range(D_FF // FFN_CHUNK):
        lo = f * FFN_CHUNK
        gt = _dot(y, wgu_ref[:, lo:lo + FFN_CHUNK])
        up = _dot(y, wgu_ref[:, D_FF + lo:D_FF + lo + FFN_CHUNK])
        act = (gt * _sigmoid(gt) * up).astype(bf16)
        out_ref[...] += _dot(act, wd_ref[lo:lo + FFN_CHUNK, :])


def _mix_ffn(h, o, w_out, gain, w_gate_up, w_down):
    m = h.shape[0]
    tm = min(ROW_TILE, m)
    row = lambda i: (i, 0)
    return pl.pallas_call(
        _mix_ffn_kernel,
        grid=(m // tm,),
        in_specs=[pl.BlockSpec((tm, D_MODEL), row), pl.BlockSpec((tm, D_MODEL), row),
                  _resident(w_out.shape), _resident((1, D_MODEL)),
                  _resident(w_gate_up.shape), _resident(w_down.shape)],
        out_specs=pl.BlockSpec((tm, D_MODEL), row),
        out_shape=jax.ShapeDtypeStruct((m, D_MODEL), f32),
        compiler_params=_params("parallel"),
        name="mix_ffn",
    )(h, o, w_out, gain.reshape(1, D_MODEL), w_gate_up, w_down)


def _fox_bias_tables():
    r = jnp.arange(3 * LANES)[:, None]
    lane = jnp.arange(LANES)[None, :]
    piece, src_head = r // LANES, r % LANES
    place_k = ((src_head < FOX_HEADS) & (lane == FOX_BIAS_LANE + 3 * src_head + piece)).astype(bf16)
    head = jnp.arange(FOX_HEADS)[:, None]
    first = FOX_BIAS_LANE + 3 * head
    const_q = -((lane >= first) & (lane < first + 3)).astype(f32)
    return place_k, const_q


def _fox_prep_kernel(q_ref, k_ref, v_ref, f_ref, bf_ref, qg_ref, kg_ref, c0_ref, pk_ref, cq_ref,
                     qx_ref, kx_ref, vx_ref, cedge_ref, carry_ref, *, tp):
    @pl.when(pl.program_id(1) == 0)
    def _():
        carry_ref[...] = c0_ref[...]

    log_f = _log_sigmoid(f_ref[...] + bf_ref[...])
    cum = _dot(_tril(tp).astype(f32), log_f, _HI) + carry_ref[...]
    carry_ref[...] = cum[tp - 1:tp, :]
    cedge_ref[0:1, :] = cum[0:1, :]
    cedge_ref[1:2, :] = cum[tp - 1:tp, :]
    c2 = cum * LOG2E
    hi = c2.astype(bf16)
    r1 = c2 - hi.astype(f32)
    mid = r1.astype(bf16)
    low = (r1 - mid.astype(f32)).astype(bf16)
    x3 = jnp.concatenate([hi, mid, low], axis=1)
    ext_k = _dot(x3, pk_ref[...])

    lo = lax.broadcasted_iota(jnp.int32, (tp, LANES), 1) < FOX_HEAD_DIM

    def head_norm(x, g):
        x2 = x * x
        s_lo = jnp.sum(jnp.where(lo, x2, 0.0), axis=-1, keepdims=True)
        s_hi = jnp.sum(jnp.where(lo, 0.0, x2), axis=-1, keepdims=True)
        ms = jnp.where(lo, s_lo, s_hi) * (1.0 / FOX_HEAD_DIM)
        return x * lax.rsqrt(ms + NORM_EPS) * g

    ones = jnp.ones((FOX_V_ROWS - FOX_HEAD_DIM, tp), bf16)
    for p in range(FOX_HEADS // 2):
        sl = slice(p * LANES, (p + 1) * LANES)
        qn = head_norm(q_ref[:, sl].astype(f32), qg_ref[...])
        kn = head_norm(k_ref[:, sl].astype(f32), kg_ref[...])
        vt = v_ref[:, sl].astype(f32).T
        halves = ((qn, kn), (pltpu.roll(qn, FOX_HEAD_DIM, axis=1), pltpu.roll(kn, FOX_HEAD_DIM, axis=1)))
        for hh, (qv, kv) in enumerate(halves):
            h = 2 * p + hh
            qx_ref[h] = jnp.where(lo, qv, cq_ref[h:h + 1, :]).astype(bf16)
            kx_ref[h] = jnp.where(lo, kv, ext_k).astype(bf16)
            vx_ref[h, 0:FOX_HEAD_DIM, :] = vt[hh * FOX_HEAD_DIM:(hh + 1) * FOX_HEAD_DIM].astype(bf16)
            vx_ref[h, FOX_HEAD_DIM:FOX_V_ROWS, :] = ones


def _fox_prep(main, small, b_f, q_gain, k_gain, c0, tables, tp):
    nb, s, _ = main.shape
    blk = lambda col: pl.BlockSpec((None, tp, D_MODEL), lambda b, i: (b, i, col))
    head_blk = pl.BlockSpec((None, FOX_HEADS, tp, LANES), lambda b, i: (b, 0, i, 0))
    return pl.pallas_call(
        functools.partial(_fox_prep_kernel, tp=tp),
        grid=(nb, s // tp),
        in_specs=[blk(0), blk(1), blk(2), pl.BlockSpec((None, tp, LANES), lambda b, i: (b, i, 0)),
                  _resident((1, LANES)), _resident((1, LANES)), _resident((1, LANES)),
                  _resident((1, LANES))] + [_resident(t.shape) for t in tables],
        out_specs=[head_blk, head_blk,
                   pl.BlockSpec((None, FOX_HEADS, None, FOX_V_ROWS, tp), lambda b, i: (b, 0, i, 0, 0)),
                   pl.BlockSpec((None, None, 2, LANES), lambda b, i: (b, i, 0, 0))],
        out_shape=[jax.ShapeDtypeStruct((nb, FOX_HEADS, s, LANES), bf16),
                   jax.ShapeDtypeStruct((nb, FOX_HEADS, s, LANES), bf16),
                   jax.ShapeDtypeStruct((nb, FOX_HEADS, s // tp, FOX_V_ROWS, tp), bf16),
                   jax.ShapeDtypeStruct((nb, s // tp, 2, LANES), f32)],
        scratch_shapes=[pltpu.VMEM((1, LANES), f32)],
        compiler_params=_params("parallel", "arbitrary"),
        name="fox_prep",
    )(main, main, main, small, b_f, q_gain, k_gain, c0, *tables)


def _fox_attn_kernel(start_ref, qx_ref, kx_ref, vx_ref, kxm_ref, vxm_ref, gate_ref, o_ref, m_ref, acc_ref,
                     s_ref, mx_ref, *, bq, nh, meta_only):
    qi = pl.program_id(2)
    heads = range(nh)
    visible = (lax.broadcasted_iota(jnp.int32, (bq, bq), 0)
               <= lax.broadcasted_iota(jnp.int32, (bq, bq), 1))

    def scores(j, diagonal):
        start = pl.multiple_of(j * bq, bq)
        ss = [_dot_nt(kx_ref[h, pl.ds(start, bq), :], qx_ref[h]) for h in heads]
        if diagonal:
            ss = [jnp.where(visible, s, -jnp.inf) for s in ss]
        return tuple(ss), tuple(jnp.max(s, axis=0, keepdims=True) for s in ss)

    def stage(blk):
        ss, mx = blk
        for h in heads:
            s_ref[h] = ss[h]
            mx_ref[h] = mx[h]

    if not meta_only:
        stage(scores(qi, True))

    sm = [_dot_nt(kxm_ref[h], qx_ref[h]) for h in heads]
    if meta_only:
        sm = [jnp.where(visible, s, -jnp.inf) for s in sm]
    ms = [jnp.max(s, axis=0, keepdims=True) for s in sm]
    ps = [jnp.exp2(s - m).astype(bf16) for s, m in zip(sm, ms)]
    for h in heads:
        m_ref[h] = ms[h]
        acc_ref[h] = _dot(vxm_ref[h], ps[h])

    if not meta_only:
        def consume(j):
            m_old = [m_ref[h] for h in heads]
            m_new = [jnp.maximum(m_old[h], mx_ref[h]) for h in heads]
            p = [jnp.exp2(s_ref[h] - m_new[h]).astype(bf16) for h in heads]
            pv = [_dot(vx_ref[h, j], p[h]) for h in heads]
            for h in heads:
                acc_ref[h] = jnp.exp2(m_old[h] - m_new[h]) * acc_ref[h] + pv[h]
                m_ref[h] = m_new[h]

        def body(j, j_prev):
            nxt = scores(j, False)
            consume(j_prev)
            stage(nxt)
            return j

        first = start_ref[pl.program_id(0), pl.program_id(1), qi]
        consume(lax.fori_loop(first, qi, body, qi))

    for pair in range(nh // 2):
        outs = []
        for h in (2 * pair, 2 * pair + 1):
            a = acc_ref[h]
            outs.append(a[0:FOX_HEAD_DIM] / a[FOX_HEAD_DIM:FOX_HEAD_DIM + 1])
        o = jnp.concatenate(outs, axis=0).T
        cols = slice(pair * LANES, (pair + 1) * LANES)
        o_ref[:, cols] = (o * _sigmoid(gate_ref[:, cols].astype(f32))).astype(bf16)


def _fox_first_block(cedge, q_gain, k_gain):
    smax = 8.0 * jnp.max(jnp.abs(q_gain)) * jnp.max(jnp.abs(k_gain)) * (LOG2E * 1.02)
    c2 = cedge[..., :FOX_HEADS] * LOG2E
    gap = c2[:, :, None, 0, :] - c2[:, None, :, 1, :]
    needed = (2.0 * smax + gap) > -FOX_ZERO_EXP
    nb, nq, nk, _ = needed.shape
    needed = needed.reshape(nb, nq, nk, FOX_HEADS // FOX_STEP_HEADS, FOX_STEP_HEADS).any(-1)
    needed = needed | (jnp.arange(nk)[None, None, :, None] >= jnp.arange(nq)[None, :, None, None])
    return jnp.argmax(needed, axis=2).astype(jnp.int32).transpose(0, 2, 1)


def _fox_attn(qx, kx, vx, kx_meta, vx_meta, main, first_block, meta_only):
    nb, _, s, _ = qx.shape
    bq = min(FOX_BLOCK, s)
    nh = FOX_STEP_HEADS
    w = nh // 2 * LANES
    gcol = 3 * D_MODEL // w
    return pl.pallas_call(
        functools.partial(_fox_attn_kernel, bq=bq, nh=nh, meta_only=meta_only),
        grid=(nb, FOX_HEADS // nh, s // bq),
        in_specs=[pl.BlockSpec(memory_space=pltpu.SMEM),
                  pl.BlockSpec((None, nh, bq, LANES), lambda b, p, i: (b, p, i, 0)),
                  pl.BlockSpec((None, nh, s, LANES), lambda b, p, i: (b, p, 0, 0)),
                  pl.BlockSpec((None, nh, s // bq, FOX_V_ROWS, bq), lambda b, p, i: (b, p, 0, 0, 0)),
                  pl.BlockSpec((None, nh, N_META, LANES), lambda b, p, i: (0, p, 0, 0)),
                  pl.BlockSpec((None, nh, None, FOX_V_ROWS, N_META), lambda b, p, i: (0, p, 0, 0, 0)),
                  pl.BlockSpec((None, bq, w), lambda b, p, i: (b, i, gcol + p))],
        out_specs=pl.BlockSpec((None, bq, w), lambda b, p, i: (b, i, p)),
        out_shape=jax.ShapeDtypeStruct((nb, s, D_MODEL), bf16),
        scratch_shapes=[pltpu.VMEM((nh, 1, bq), f32), pltpu.VMEM((nh, FOX_V_ROWS, bq), f32),
                        pltpu.VMEM((nh, bq, bq), f32), pltpu.VMEM((nh, 1, bq), f32)],
        compiler_params=_params("parallel", "parallel", "arbitrary"),
        name="fox_attn",
    )(first_block, qx, kx, vx, kx_meta, vx_meta, main)


def _gla_kernel(q_ref, k_ref, v_ref, r_ref, a_ref, w2_ref, ba_ref, og_ref, s0_ref,
                o_ref, sout_ref, st_ref, b_sc, k_sc, *, C, nc):
    c = pl.program_id(1)
    T = nc * C

    @pl.when(c == 0)
    def _():
        st_ref[...] = s0_ref[...]

    g = _log_sigmoid(_dot(a_ref[...], w2_ref[...], _HI) + ba_ref[...]) * (1.0 / GLA_GATE_NORMALIZER)
    b_sc[...] = _chunk_cumsum(g, C)
    k_sc[...] = k_ref[...].astype(f32)

    nb = C // SUB_BLOCK
    trow = lax.broadcasted_iota(jnp.int32, (SUB_BLOCK, 1), 0)
    lane = lax.broadcasted_iota(jnp.int32, (SUB_BLOCK, SUB_BLOCK), 1)

    units = [(ck, h) for ck in range(nc) for h in range(GLA_HEADS)]
    bhs, qhs, khs, vhs, a_diag, a_off = [], [], [], [], [], []
    for ck, h in units:
        tok = slice(ck * C, (ck + 1) * C)
        sl = slice(h * GLA_DK, (h + 1) * GLA_DK)
        bh = b_sc[tok, sl]
        qh = q_ref[tok, sl].astype(f32) * (GLA_DK ** -0.5)
        kh = k_sc[tok, sl]
        diag, off = [], []
        for i in range(nb):
            r0 = i * SUB_BLOCK
            bi = bh[r0:r0 + SUB_BLOCK]
            qi = qh[r0:r0 + SUB_BLOCK]
            ad = jnp.zeros((SUB_BLOCK, SUB_BLOCK), f32)
            for s in range(SUB_BLOCK):
                row = ck * C + r0 + s
                bs = b_sc[row:row + 1, sl]
                ks = k_sc[row:row + 1, sl]
                e = jnp.exp(jnp.where(trow >= s, bi - bs, -jnp.inf))
                col = jnp.sum(qi * ks * e, axis=-1, keepdims=True)
                ad = jnp.where(lane == s, col, ad)
            diag.append(ad.astype(bf16))
            if i > 0:
                ref = bi[0:1, :]
                q_rel = (qi * jnp.exp(bi - ref)).astype(bf16)
                k_rel = (kh[:r0] * jnp.exp(ref - bh[:r0])).astype(bf16)
                off.append(_dot_nt(q_rel, k_rel).astype(bf16))
        bhs.append(bh)
        qhs.append(qh)
        khs.append(kh)
        vhs.append(v_ref[tok, h * GLA_DV:(h + 1) * GLA_DV])
        a_diag.append(diag)
        a_off.append(off)

    o_intra = []
    for u in range(len(units)):
        rows = []
        for i in range(nb):
            r0 = i * SUB_BLOCK
            o_i = _dot(a_diag[u][i], vhs[u][r0:r0 + SUB_BLOCK])
            if i > 0:
                o_i = o_i + _dot(a_off[u][i - 1], vhs[u][:r0])
            rows.append(o_i)
        o_intra.append(jnp.concatenate(rows, axis=0) if nb > 1 else rows[0])

    states = [st_ref[h] for h in range(GLA_HEADS)]
    for u, (ck, h) in enumerate(units):
        bh, st = bhs[u], states[h]
        blast = bh[C - 1:C, :]
        o = o_intra[u] + _dot_nt((qhs[u] * jnp.exp(bh)).astype(bf16), st.astype(bf16))
        k_dec = (khs[u] * jnp.exp(blast - bh)).astype(bf16)
        states[h] = st * jnp.exp(blast) + _dot_tn(vhs[u], k_dec)
        tok = slice(ck * C, (ck + 1) * C)
        vs = slice(h * GLA_DV, (h + 1) * GLA_DV)
        r = r_ref[tok, vs].astype(f32)
        o_ref[tok, vs] = (_rmsnorm(o, og_ref[...]) * (r * _sigmoid(r))).astype(bf16)
    for h in range(GLA_HEADS):
        st_ref[h] = states[h]

    @pl.when(c == pl.num_programs(1) - 1)
    def _():
        sout_ref[...] = st_ref[...]


def _gla_mix(main, small, w_alpha2, b_alpha, o_gain, s0, C):
    nb, s, _ = main.shape
    nc = min(GLA_STEP_CHUNKS, s // C)
    t = nc * C
    blk = lambda w, col: pl.BlockSpec((None, t, w), lambda b, c: (b, c, col))
    state = (GLA_HEADS, GLA_DV, GLA_DK)
    return pl.pallas_call(
        functools.partial(_gla_kernel, C=C, nc=nc),
        grid=(nb, s // t),
        in_specs=[blk(GLA_QK, 0), blk(GLA_QK, 1), blk(GLA_V, 1), blk(GLA_V, 2),
                  pl.BlockSpec((None, t, LANES), lambda b, c: (b, c, 0)),
                  _resident((LANES, GLA_QK)), _resident((1, GLA_QK)), _resident((1, GLA_DV)),
                  pl.BlockSpec((None,) + state, lambda b, c: (0, 0, 0, 0))],
        out_specs=[blk(GLA_V, 0), pl.BlockSpec((None,) + state, lambda b, c: (b, 0, 0, 0))],
        out_shape=[jax.ShapeDtypeStruct((nb, s, GLA_V), bf16),
                   jax.ShapeDtypeStruct((nb,) + state, f32)],
        scratch_shapes=[pltpu.VMEM(state, f32), pltpu.VMEM((t, GLA_QK), f32),
                        pltpu.VMEM((t, GLA_QK), f32)],
        compiler_params=_params("parallel", "arbitrary"),
        name="gla_mix",
    )(main, main, main, main, small, w_alpha2, b_alpha, o_gain, s0)


def _unit_lower_inverses(lowers, n, span):
    r = lax.broadcasted_iota(jnp.int32, (n, n), 0)
    c = lax.broadcasted_iota(jnp.int32, (n, n), 1)
    eye = (r == c).astype(f32)
    same_block = (r // SUB_BLOCK) == (c // SUB_BLOCK)
    ms = [jnp.where(same_block, -lower, 0.0) for lower in lowers]
    ps = [eye + m for m in ms]
    for _ in range(3):
        ms = [_bdot(m, m) for m in ms]
        ps = [p + _bdot(p, m) for p, m in zip(ps, ms)]
    if span == SUB_BLOCK:
        return ps
    assert span // SUB_BLOCK <= 4
    es = [_bdot(p, jnp.where(same_block, 0.0, lower)) for p, lower in zip(ps, lowers)]
    e2s = [_bdot(e, e) for e in es]
    xs = [eye - e + e2 - _bdot(e, e2) for e, e2 in zip(es, e2s)]
    return [_bdot(x, p) for x, p in zip(xs, ps)]


def _gdn_kernel(x_ref, gate_ref, sm_ref, cw_ref, alog_ref, dtb_ref, og_ref, tail0_ref, s0_ref,
                o_ref, sout_ref, xbuf, s_ref, *, C, nc):
    c = pl.program_id(1)
    T = nc * C

    @pl.when(c == 0)
    def _():
        xbuf[0:8, :] = tail0_ref[...]
        s_ref[...] = s0_ref[...]

    xbuf[8:8 + T, :] = x_ref[...].astype(f32)

    def conv_silu(col):
        sl = slice(col, col + LANES)
        y = cw_ref[0:1, sl] * xbuf[5:5 + T, sl]
        for j in range(1, GDN_CONV):
            y = y + cw_ref[j:j + 1, sl] * xbuf[5 + j:5 + j + T, sl]
        return y * _sigmoid(y)

    def l2norm(x):
        return x * lax.rsqrt(jnp.sum(x * x, axis=-1, keepdims=True) + NORM_EPS)

    sm = sm_ref[...]
    g_all = -jnp.exp(alog_ref[...]) * _softplus(sm + dtb_ref[...])
    beta_all = _sigmoid(sm)
    tr = lax.broadcasted_iota(jnp.int32, (T, T), 0)
    tc = lax.broadcasted_iota(jnp.int32, (T, T), 1)
    in_chunk_tril = (((tr // C) == (tc // C)) & (tr >= tc)).astype(f32)
    b_all = _dot(in_chunk_tril, g_all, _HI)
    hd = GDN_HEADS * GDN_DK
    n = GDN_GROUP * C

    rr = lax.broadcasted_iota(jnp.int32, (n, n), 0)
    cc = lax.broadcasted_iota(jnp.int32, (n, n), 1)
    strict = ((rr // C) == (cc // C)) & (rr > cc)
    eye = (rr == cc).astype(f32)

    groups = [range(g * GDN_GROUP, (g + 1) * GDN_GROUP) for g in range(GDN_HEADS // GDN_GROUP)]
    units = [(ck, g) for ck in range(nc) for g in range(len(groups))]
    rows = [slice(i * C, (i + 1) * C) for i in range(GDN_GROUP)]

    def stack(per_head):
        return [jnp.concatenate([per_head[h][ck * C:(ck + 1) * C] for h in groups[g]], axis=0)
                for ck, g in units]

    every = range(GDN_HEADS)
    q_st = stack([l2norm(conv_silu(h * GDN_DK)) * (GDN_DK ** -0.5) for h in every])
    k_st = stack([l2norm(conv_silu(hd + h * GDN_DK)) for h in every])
    v_st = stack([conv_silu(2 * hd + h * GDN_DV) for h in every])
    b_st = stack([b_all[:, h:h + 1] for h in every])
    beta_st = stack([beta_all[:, GDN_HEADS + h:GDN_HEADS + h + 1] for h in every])
    b_row = [jnp.broadcast_to(b, (n, LANES)).T[0:1, :] for b in b_st]

    dec = [jnp.exp(jnp.where(strict, b - br, -jnp.inf)) for b, br in zip(b_st, b_row)]
    kb_st = [k * beta for k, beta in zip(k_st, beta_st)]
    k16 = [k.astype(bf16) for k in k_st]
    lower = [_dot_nt(kb.astype(bf16), k) * d for kb, k, d in zip(kb_st, k16, dec)]
    attn = [(_dot_nt(q.astype(bf16), k) * (d + eye)).astype(bf16) for q, k, d in zip(q_st, k16, dec)]
    t_inv = _unit_lower_inverses(lower, n, C)
    e_b = [jnp.exp(b) for b in b_st]
    sol = [_bdot(t, jnp.concatenate([v * beta, kb * e], axis=1))
           for t, v, beta, kb, e in zip(t_inv, v_st, beta_st, kb_st, e_b)]
    qe_st = [(q * e).astype(bf16) for q, e in zip(q_st, e_b)]

    states = [s_ref[h] for h in every]
    for u, (ck, g) in enumerate(units):
        heads = groups[g]
        s16 = [states[h].astype(bf16) for h in heads]
        v_new = jnp.concatenate([sol[u][rows[i], :GDN_DV] - _dot(sol[u][rows[i], GDN_DV:].astype(bf16), s16[i])
                                 for i in range(GDN_GROUP)], axis=0)
        vn16 = v_new.astype(bf16)
        o_u = _dot(attn[u], vn16) + jnp.concatenate(
            [_dot(qe_st[u][rows[i]], s16[i]) for i in range(GDN_GROUP)], axis=0)
        for i, h in enumerate(heads):
            b_h = b_st[u][rows[i]]
            b_last = b_h[C - 1:C, :]
            k_dec = (k_st[u][rows[i]] * jnp.exp(b_last - b_h)).astype(bf16)
            states[h] = states[h] * jnp.exp(b_last) + _dot_tn(k_dec, vn16[rows[i]])
            vs = slice(h * GDN_DV, (h + 1) * GDN_DV)
            tok = slice(ck * C, (ck + 1) * C)
            gt = gate_ref[tok, vs].astype(f32)
            o_ref[tok, vs] = (_rmsnorm(o_u[rows[i]], og_ref[...]) * (gt * _sigmoid(gt))).astype(bf16)
    for h in every:
        s_ref[h] = states[h]

    xbuf[0:8, :] = xbuf[T:T + 8, :]

    @pl.when(c == pl.num_programs(1) - 1)
    def _():
        sout_ref[...] = s_ref[...]


def _gdn_mix(main, small, conv_w, a_log, dt_bias, o_gain, tail0, s0, C):
    nb, s, _ = main.shape
    state = (GDN_HEADS, GDN_DK, GDN_DV)
    nc = min(GDN_STEP_CHUNKS, s // C)
    t = nc * C
    return pl.pallas_call(
        functools.partial(_gdn_kernel, C=C, nc=nc),
        grid=(nb, s // t),
        in_specs=[pl.BlockSpec((None, t, GDN_QKV), lambda b, c: (b, c, 0)),
                  pl.BlockSpec((None, t, D_MODEL), lambda b, c: (b, c, GDN_QKV // D_MODEL)),
                  pl.BlockSpec((None, t, LANES), lambda b, c: (b, c, 0)),
                  _resident((GDN_CONV, GDN_QKV)), _resident((1, LANES)), _resident((1, LANES)),
                  _resident((1, GDN_DV)), _resident((8, GDN_QKV)),
                  pl.BlockSpec((None,) + state, lambda b, c: (0, 0, 0, 0))],
        out_specs=[pl.BlockSpec((None, t, D_MODEL), lambda b, c: (b, c, 0)),
                   pl.BlockSpec((None,) + state, lambda b, c: (b, 0, 0, 0))],
        out_shape=[jax.ShapeDtypeStruct((nb, s, D_MODEL), bf16),
                   jax.ShapeDtypeStruct((nb,) + state, f32)],
        scratch_shapes=[pltpu.VMEM((8 + t, GDN_QKV), f32), pltpu.VMEM(state, f32)],
        compiler_params=_params("parallel", "arbitrary"),
        name="gdn_mix",
    )(main, main, small, conv_w, a_log, dt_bias, o_gain, tail0, s0)


def _split_in_proj(w_in, n_main):
    w_small = jnp.pad(w_in[:, n_main:], ((0, 0), (0, LANES - (w_in.shape[1] - n_main))))
    return w_in[:, :n_main].astype(bf16), w_small.astype(bf16)


def _lane_row(v, width=LANES, offset=0):
    return jnp.pad(v.astype(f32), (offset, width - offset - v.shape[0])).reshape(1, width)


def kernel(x, meta_tokens, norm_mix, norm_ffn, w_gate_up, w_down, fox_w_in, fox_b_f, fox_q_gain, fox_k_gain, fox_w_out, gla_w_in, gla_w_alpha2, gla_b_alpha, gla_o_gain, gla_w_out, gdn_w_in, gdn_conv_w, gdn_a_log, gdn_dt_bias, gdn_o_gain, gdn_w_out):
    nb, seq, _ = x.shape
    depth = norm_mix.shape[0]
    hr = x.reshape(nb * seq, D_MODEL)
    hm = meta_tokens.astype(x.dtype)
    fox_tables = _fox_bias_tables()

    for i in range(depth):
        kind, j = i % 3, i // 3
        if kind == 0:
            w_main, w_small = _split_in_proj(fox_w_in[j], 4 * D_MODEL)
            w_out = fox_w_out[j]
            main_m, small_m = _norm_proj(hm, norm_mix[i], w_main, w_small)
            main_r, small_r = _norm_proj(hr, norm_mix[i], w_main, w_small)
            main_m = main_m.reshape(1, N_META, -1)
            main_r = main_r.reshape(nb, seq, -1)
            b_f = _lane_row(fox_b_f[j])
            qg = jnp.tile(fox_q_gain[j], 2).reshape(1, LANES) * (FOX_HEAD_DIM ** -0.5 * LOG2E)
            kg = jnp.tile(fox_k_gain[j], 2).reshape(1, LANES)
            qx_m, kx_m, vx_m, c_m = _fox_prep(main_m, small_m.reshape(1, N_META, LANES), b_f, qg, kg,
                                              jnp.zeros((1, LANES), f32), fox_tables, N_META)
            qx_r, kx_r, vx_r, c_r = _fox_prep(main_r, small_r.reshape(nb, seq, LANES), b_f, qg, kg,
                                              c_m[0, 0, 1:2, :], fox_tables, min(FOX_BLOCK, seq))
            groups = FOX_HEADS // FOX_STEP_HEADS
            o_m = _fox_attn(qx_m, kx_m, vx_m, kx_m, vx_m, main_m, jnp.zeros((1, groups, 1), jnp.int32), True)
            o_r = _fox_attn(qx_r, kx_r, vx_r, kx_m, vx_m, main_r,
                            _fox_first_block(c_r, fox_q_gain[j], fox_k_gain[j]), False)
        elif kind == 1:
            w_main, w_small = _split_in_proj(gla_w_in[j], 2 * GLA_QK + 2 * GLA_V)
            w_out = gla_w_out[j]
            main_m, small_m = _norm_proj(hm, norm_mix[i], w_main, w_small)
            main_r, small_r = _norm_proj(hr, norm_mix[i], w_main, w_small)
            w2 = jnp.pad(gla_w_alpha2[j], ((0, LANES - gla_w_alpha2.shape[1]), (0, 0)))
            ba = gla_b_alpha[j].reshape(1, GLA_QK)
            og = gla_o_gain[j].reshape(1, GLA_DV)
            s0 = jnp.zeros((1, GLA_HEADS, GLA_DV, GLA_DK), f32)
            o_m, s_m = _gla_mix(main_m.reshape(1, N_META, -1), small_m.reshape(1, N_META, LANES),
                                w2, ba, og, s0, N_META)
            o_r, _ = _gla_mix(main_r.reshape(nb, seq, -1), small_r.reshape(nb, seq, LANES),
                              w2, ba, og, s_m, GLA_CHUNK)
        else:
            w_main, w_small = _split_in_proj(gdn_w_in[j], GDN_QKV + GDN_HEADS * GDN_DV)
            w_out = gdn_w_out[j]
            main_m, small_m = _norm_proj(hm, norm_mix[i], w_main, w_small)
            main_r, small_r = _norm_proj(hr, norm_mix[i], w_main, w_small)
            cw = gdn_conv_w[j].reshape(GDN_CONV, GDN_QKV)
            alog = _lane_row(gdn_a_log[j])
            dtb = _lane_row(gdn_dt_bias[j])
            og = gdn_o_gain[j].reshape(1, GDN_DV)
            s0 = jnp.zeros((1, GDN_HEADS, GDN_DK, GDN_DV), f32)
            o_m, s_m = _gdn_mix(main_m.reshape(1, N_META, -1), small_m.reshape(1, N_META, LANES),
                                cw, alog, dtb, og, jnp.zeros((8, GDN_QKV), f32), s0, N_META)
            tail = main_m[N_META - 8:, :GDN_QKV].astype(f32)
            o_r, _ = _gdn_mix(main_r.reshape(nb, seq, -1), small_r.reshape(nb, seq, LANES),
                              cw, alog, dtb, og, tail, s_m, GDN_CHUNK)

        wo = w_out.astype(bf16)
        wgu = w_gate_up[i].astype(bf16)
        wd = w_down[i].astype(bf16)
        hm = _mix_ffn(hm, o_m.reshape(N_META, D_MODEL), wo, norm_ffn[i], wgu, wd)
        hr = _mix_ffn(hr, o_r.reshape(nb * seq, D_MODEL), wo, norm_ffn[i], wgu, wd)

    return hr.reshape(nb, seq, D_MODEL)
```

```python
import functools
import math

import jax
import jax.numpy as jnp
from jax import lax
from jax.experimental import pallas as pl
from jax.experimental.pallas import tpu as pltpu

f32 = jnp.float32
bf16 = jnp.bfloat16

D_MODEL = 1024
N_META = 16
NORM_EPS = 1e-6
LANES = 128
SUB_BLOCK = 16
LOG2E = math.log2(math.e)

FOX_HEADS = 16
FOX_HEAD_DIM = 64
FOX_BLOCK = 256
FOX_STEP_HEADS = 8
FOX_ZERO_EXP = 160.0
FOX_V_ROWS = FOX_HEAD_DIM + 16
FOX_BIAS_LANE = FOX_HEAD_DIM

GLA_HEADS = 4
GLA_DK = 128
GLA_DV = 256
GLA_QK = GLA_HEADS * GLA_DK
GLA_V = GLA_HEADS * GLA_DV
GLA_GATE_NORMALIZER = 16.0
GLA_CHUNK = 64
GLA_STEP_CHUNKS = 4

GDN_HEADS = 8
GDN_DK = 128
GDN_DV = 128
GDN_CONV = 4
GDN_CHUNK = 64
GDN_QKV = 3 * GDN_HEADS * GDN_DK
GDN_GROUP = 2
GDN_STEP_CHUNKS = 4

D_FF = 2816
FFN_CHUNK = 256

ROW_TILE = 1024
VMEM_LIMIT = 56 * 2 ** 20

_HI = lax.Precision.HIGHEST


def _params(*sem):
    return pltpu.CompilerParams(dimension_semantics=sem, vmem_limit_bytes=VMEM_LIMIT)


def _resident(shape):
    nd = len(shape)
    return pl.BlockSpec(shape, lambda *_: (0,) * nd, pipeline_mode=pl.Buffered(1))


def _dot(a, b, precision=None):
    return jnp.dot(a, b, preferred_element_type=f32, precision=precision)


def _dot_nt(a, b, precision=None):
    return lax.dot_general(a, b, (((1,), (1,)), ((), ())), preferred_element_type=f32,
                           precision=precision)


def _dot_tn(a, b, precision=None):
    return lax.dot_general(a, b, (((0,), (0,)), ((), ())), preferred_element_type=f32,
                           precision=precision)


def _bdot(a, b):
    return _dot(a.astype(bf16), b.astype(bf16))


def _sigmoid(x):
    return 1.0 / (1.0 + jnp.exp(-x))


def _softplus(x):
    return jnp.maximum(x, 0.0) + jnp.log(1.0 + jnp.exp(-jnp.abs(x)))


def _log_sigmoid(x):
    return -_softplus(-x)


def _rmsnorm(x, g):
    return x * lax.rsqrt(jnp.mean(x * x, axis=-1, keepdims=True) + NORM_EPS) * g


def _tril(n, k=0):
    r = lax.broadcasted_iota(jnp.int32, (n, n), 0)
    c = lax.broadcasted_iota(jnp.int32, (n, n), 1)
    return r + k >= c


def _chunk_cumsum(x, chunk):
    pos = lax.broadcasted_iota(jnp.int32, (x.shape[0], 1), 0) % chunk
    d = 1
    while d < chunk:
        x = x + jnp.where(pos >= d, pltpu.roll(x, d, axis=0), 0.0)
        d *= 2
    return x


def _norm_proj_kernel(h_ref, g_ref, w_ref, ws_ref, o_ref, os_ref, *, tn):
    y = _rmsnorm(h_ref[...], g_ref[...]).astype(bf16)
    os_ref[...] = _dot(y, ws_ref[...])
    for j in range(w_ref.shape[1] // tn):
        o_ref[:, j * tn:(j + 1) * tn] = _dot(y, w_ref[:, j * tn:(j + 1) * tn]).astype(bf16)


def _norm_proj(h, gain, w_main, w_small):
    m = h.shape[0]
    tm = min(ROW_TILE, m)
    n = w_main.shape[1]
    return pl.pallas_call(
        functools.partial(_norm_proj_kernel, tn=512),
        grid=(m // tm,),
        in_specs=[pl.BlockSpec((tm, D_MODEL), lambda i: (i, 0)),
                  _resident((1, D_MODEL)), _resident(w_main.shape), _resident(w_small.shape)],
        out_specs=[pl.BlockSpec((tm, n), lambda i: (i, 0)),
                   pl.BlockSpec((tm, LANES), lambda i: (i, 0))],
        out_shape=[jax.ShapeDtypeStruct((m, n), bf16), jax.ShapeDtypeStruct((m, LANES), f32)],
        compiler_params=_params("parallel"),
        name="norm_proj",
    )(h, gain.reshape(1, D_MODEL), w_main, w_small)


def _mix_ffn_kernel(h_ref, o_ref, wo_ref, g_ref, wgu_ref, wd_ref, out_ref):
    hmid = h_ref[...] + _dot(o_ref[...], wo_ref[...])
    y = _rmsnorm(hmid, g_ref[...]).astype(bf16)
    out_ref[...] = hmid
    for f in range(D_FF // FFN_CHUNK):
        lo = f * FFN_CHUNK
        gt = _dot(y, wgu_ref[:, lo:lo + FFN_CHUNK])
        up = _dot(y, wgu_ref[:, D_FF + lo:D_FF + lo + FFN_CHUNK])
        act = (gt * _sigmoid(gt) * up).astype(bf16)
        out_ref[...] += _dot(act, wd_ref[lo:lo + FFN_CHUNK, :])


def _mix_ffn(h, o, w_out, gain, w_gate_up, w_down):
    m = h.shape[0]
    tm = min(ROW_TILE, m)
    row = lambda i: (i, 0)
    return pl.pallas_call(
        _mix_ffn_kernel,
        grid=(m // tm,),
        in_specs=[pl.BlockSpec((tm, D_MODEL), row), pl.BlockSpec((tm, D_MODEL), row),
                  _resident(w_out.shape), _resident((1, D_MODEL)),
                  _resident(w_gate_up.shape), _resident(w_down.shape)],
        out_specs=pl.BlockSpec((tm, D_MODEL), row),
        out_shape=jax.ShapeDtypeStruct((m, D_MODEL), f32),
        compiler_params=_params("parallel"),
        name="mix_ffn",
    )(h, o, w_out, gain.reshape(1, D_MODEL), w_gate_up, w_down)


def _fox_bias_tables():
    r = jnp.arange(3 * LANES)[:, None]
    lane = jnp.arange(LANES)[None, :]
    piece, src_head = r // LANES, r % LANES
    place_k = ((src_head < FOX_HEADS) & (lane == FOX_BIAS_LANE + 3 * src_head + piece)).astype(bf16)
    head = jnp.arange(FOX_HEADS)[:, None]
    first = FOX_BIAS_LANE + 3 * head
    const_q = -((lane >= first) & (lane < first + 3)).astype(f32)
    r2 = jnp.arange(2 * LANES)[:, None] % LANES
    head_sum = ((r2 // FOX_HEAD_DIM) == (lane // FOX_HEAD_DIM)).astype(bf16)
    return place_k, const_q, head_sum


def _fox_prep_kernel(q_ref, k_ref, v_ref, f_ref, bf_ref, qg_ref, kg_ref, c0_ref, pk_ref, cq_ref, hs_ref,
                     qx_ref, kx_ref, vx_ref, cedge_ref, carry_ref, *, tp):
    @pl.when(pl.program_id(1) == 0)
    def _():
        carry_ref[...] = c0_ref[...]

    log_f = _log_sigmoid(f_ref[...] + bf_ref[...])
    cum = _dot(_tril(tp).astype(f32), log_f, _HI) + carry_ref[...]
    carry_ref[...] = cum[tp - 1:tp, :]
    cedge_ref[0:1, :] = cum[0:1, :]
    cedge_ref[1:2, :] = cum[tp - 1:tp, :]
    c2 = cum * LOG2E
    hi = c2.astype(bf16)
    r1 = c2 - hi.astype(f32)
    mid = r1.astype(bf16)
    low = (r1 - mid.astype(f32)).astype(bf16)
    x3 = jnp.concatenate([hi, mid, low], axis=1)
    ext_k = _dot(x3, pk_ref[...])

    lo = lax.broadcasted_iota(jnp.int32, (tp, LANES), 1) < FOX_HEAD_DIM

    def head_norm(x, g):
        x2 = x * x
        hi2 = x2.astype(bf16)
        lo2 = (x2 - hi2.astype(f32)).astype(bf16)
        ss = _dot(jnp.concatenate([hi2, lo2], axis=1), hs_ref[...])
        return x * lax.rsqrt(ss * (1.0 / FOX_HEAD_DIM) + NORM_EPS) * g

    ones = jnp.ones((FOX_V_ROWS - FOX_HEAD_DIM, tp), bf16)
    for p in range(FOX_HEADS // 2):
        sl = slice(p * LANES, (p + 1) * LANES)
        qn = head_norm(q_ref[:, sl].astype(f32), qg_ref[...])
        kn = head_norm(k_ref[:, sl].astype(f32), kg_ref[...])
        vt = v_ref[:, sl].astype(f32).T
        halves = ((qn, kn), (pltpu.roll(qn, FOX_HEAD_DIM, axis=1), pltpu.roll(kn, FOX_HEAD_DIM, axis=1)))
        for hh, (qv, kv) in enumerate(halves):
            h = 2 * p + hh
            qx_ref[h] = jnp.where(lo, qv, cq_ref[h:h + 1, :]).astype(bf16)
            kx_ref[h] = jnp.where(lo, kv, ext_k).astype(bf16)
            vx_ref[h, 0:FOX_HEAD_DIM, :] = vt[hh * FOX_HEAD_DIM:(hh + 1) * FOX_HEAD_DIM].astype(bf16)
            vx_ref[h, FOX_HEAD_DIM:FOX_V_ROWS, :] = ones


def _fox_prep(main, small, b_f, q_gain, k_gain, c0, tables, tp):
    nb, s, _ = main.shape
    blk = lambda col: pl.BlockSpec((None, tp, D_MODEL), lambda b, i: (b, i, col))
    head_blk = pl.BlockSpec((None, FOX_HEADS, tp, LANES), lambda b, i: (b, 0, i, 0))
    return pl.pallas_call(
        functools.partial(_fox_prep_kernel, tp=tp),
        grid=(nb, s // tp),
        in_specs=[blk(0), blk(1), blk(2), pl.BlockSpec((None, tp, LANES), lambda b, i: (b, i, 0)),
                  _resident((1, LANES)), _resident((1, LANES)), _resident((1, LANES)),
                  _resident((1, LANES))] + [_resident(t.shape) for t in tables],
        out_specs=[head_blk, head_blk,
                   pl.BlockSpec((None, FOX_HEADS, None, FOX_V_ROWS, tp), lambda b, i: (b, 0, i, 0, 0)),
                   pl.BlockSpec((None, None, 2, LANES), lambda b, i: (b, i, 0, 0))],
        out_shape=[jax.ShapeDtypeStruct((nb, FOX_HEADS, s, LANES), bf16),
                   jax.ShapeDtypeStruct((nb, FOX_HEADS, s, LANES), bf16),
                   jax.ShapeDtypeStruct((nb, FOX_HEADS, s // tp, FOX_V_ROWS, tp), bf16),
                   jax.ShapeDtypeStruct((nb, s // tp, 2, LANES), f32)],
        scratch_shapes=[pltpu.VMEM((1, LANES), f32)],
        compiler_params=_params("parallel", "arbitrary"),
        name="fox_prep",
    )(main, main, main, small, b_f, q_gain, k_gain, c0, *tables)


def _fox_attn_kernel(start_ref, qx_ref, kx_ref, vx_ref, kxm_ref, vxm_ref, gate_ref, o_ref, m_ref, acc_ref,
                     s_ref, mx_ref, *, bq, nh, meta_only):
    qi = pl.program_id(2)
    heads = range(nh)
    visible = (lax.broadcasted_iota(jnp.int32, (bq, bq), 0)
               <= lax.broadcasted_iota(jnp.int32, (bq, bq), 1))

    def scores(j, diagonal):
        start = pl.multiple_of(j * bq, bq)
        ss = [_dot_nt(kx_ref[h, pl.ds(start, bq), :], qx_ref[h]) for h in heads]
        if diagonal:
            ss = [jnp.where(visible, s, -jnp.inf) for s in ss]
        return tuple(ss), tuple(jnp.max(s, axis=0, keepdims=True) for s in ss)

    def stage(blk):
        ss, mx = blk
        for h in heads:
            s_ref[h] = ss[h]
            mx_ref[h] = mx[h]

    sm = [_dot_nt(kxm_ref[h], qx_ref[h]) for h in heads]
    if meta_only:
        sm = [jnp.where(visible, s, -jnp.inf) for s in sm]

    if not meta_only:
        stage(scores(qi, True))

    ms = [jnp.max(s, axis=0, keepdims=True) for s in sm]
    ps = [jnp.exp2(s - m).astype(bf16) for s, m in zip(sm, ms)]
    for h in heads:
        m_ref[h] = ms[h]
        acc_ref[h] = _dot(vxm_ref[h], ps[h])

    if not meta_only:
        def consume(j):
            m_old = [m_ref[h] for h in heads]
            m_new = [jnp.maximum(m_old[h], mx_ref[h]) for h in heads]
            p = [jnp.exp2(s_ref[h] - m_new[h]).astype(bf16) for h in heads]
            pv = [_dot(vx_ref[h, j], p[h]) for h in heads]
            for h in heads:
                acc_ref[h] = jnp.exp2(m_old[h] - m_new[h]) * acc_ref[h] + pv[h]
                m_ref[h] = m_new[h]

        def body(j, j_prev):
            nxt = scores(j, False)
            consume(j_prev)
            stage(nxt)
            return j

        first = start_ref[pl.program_id(0), pl.program_id(1), qi]
        consume(lax.fori_loop(first, qi, body, qi))

    for pair in range(nh // 2):
        outs = []
        for h in (2 * pair, 2 * pair + 1):
            a = acc_ref[h]
            outs.append(a[0:FOX_HEAD_DIM] / a[FOX_HEAD_DIM:FOX_HEAD_DIM + 1])
        o = jnp.concatenate(outs, axis=0).T
        cols = slice(pair * LANES, (pair + 1) * LANES)
        o_ref[:, cols] = (o * _sigmoid(gate_ref[:, cols].astype(f32))).astype(bf16)


def _fox_first_block(cedge, q_gain, k_gain):
    smax = 8.0 * jnp.max(jnp.abs(q_gain)) * jnp.max(jnp.abs(k_gain)) * (LOG2E * 1.02)
    c2 = cedge[..., :FOX_HEADS] * LOG2E
    gap = c2[:, :, None, 0, :] - c2[:, None, :, 1, :]
    needed = (2.0 * smax + gap) > -FOX_ZERO_EXP
    nb, nq, nk, _ = needed.shape
    needed = needed.reshape(nb, nq, nk, FOX_HEADS // FOX_STEP_HEADS, FOX_STEP_HEADS).any(-1)
    needed = needed | (jnp.arange(nk)[None, None, :, None] >= jnp.arange(nq)[None, :, None, None])
    return jnp.argmax(needed, axis=2).astype(jnp.int32).transpose(0, 2, 1)


def _fox_attn(qx, kx, vx, kx_meta, vx_meta, main, first_block, meta_only):
    nb, _, s, _ = qx.shape
    bq = min(FOX_BLOCK, s)
    nh = FOX_STEP_HEADS
    w = nh // 2 * LANES
    gcol = 3 * D_MODEL // w
    return pl.pallas_call(
        functools.partial(_fox_attn_kernel, bq=bq, nh=nh, meta_only=meta_only),
        grid=(nb, FOX_HEADS // nh, s // bq),
        in_specs=[pl.BlockSpec(memory_space=pltpu.SMEM),
                  pl.BlockSpec((None, nh, bq, LANES), lambda b, p, i: (b, p, i, 0)),
                  pl.BlockSpec((None, nh, s, LANES), lambda b, p, i: (b, p, 0, 0)),
                  pl.BlockSpec((None, nh, s // bq, FOX_V_ROWS, bq), lambda b, p, i: (b, p, 0, 0, 0)),
                  pl.BlockSpec((None, nh, N_META, LANES), lambda b, p, i: (0, p, 0, 0)),
                  pl.BlockSpec((None, nh, None, FOX_V_ROWS, N_META), lambda b, p, i: (0, p, 0, 0, 0)),
                  pl.BlockSpec((None, bq, w), lambda b, p, i: (b, i, gcol + p))],
        out_specs=pl.BlockSpec((None, bq, w), lambda b, p, i: (b, i, p)),
        out_shape=jax.ShapeDtypeStruct((nb, s, D_MODEL), bf16),
        scratch_shapes=[pltpu.VMEM((nh, 1, bq), f32), pltpu.VMEM((nh, FOX_V_ROWS, bq), f32),
                        pltpu.VMEM((nh, bq, bq), f32), pltpu.VMEM((nh, 1, bq), f32)],
        compiler_params=_params("parallel", "parallel", "arbitrary"),
        name="fox_attn",
    )(first_block, qx, kx, vx, kx_meta, vx_meta, main)


def _gla_kernel(q_ref, k_ref, v_ref, r_ref, a_ref, w2_ref, ba_ref, og_ref, s0_ref,
                o_ref, sout_ref, st_ref, b_sc, k_sc, *, C, nc):
    c = pl.program_id(1)
    T = nc * C

    @pl.when(c == 0)
    def _():
        st_ref[...] = s0_ref[...]

    g = _log_sigmoid(_dot(a_ref[...], w2_ref[...], _HI) + ba_ref[...]) * (1.0 / GLA_GATE_NORMALIZER)
    b_sc[...] = _chunk_cumsum(g, C)
    k_sc[...] = k_ref[...].astype(f32)

    nb = C // SUB_BLOCK
    trow = lax.broadcasted_iota(jnp.int32, (SUB_BLOCK, 1), 0)
    lane = lax.broadcasted_iota(jnp.int32, (SUB_BLOCK, SUB_BLOCK), 1)

    units = [(ck, h) for ck in range(nc) for h in range(GLA_HEADS)]
    bhs, qhs, khs, vhs, a_diag, a_off = [], [], [], [], [], []
    for ck, h in units:
        tok = slice(ck * C, (ck + 1) * C)
        sl = slice(h * GLA_DK, (h + 1) * GLA_DK)
        bh = b_sc[tok, sl]
        qh = q_ref[tok, sl].astype(f32) * (GLA_DK ** -0.5)
        kh = k_sc[tok, sl]
        diag, off = [], []
        for i in range(nb):
            r0 = i * SUB_BLOCK
            bi = bh[r0:r0 + SUB_BLOCK]
            qi = qh[r0:r0 + SUB_BLOCK]
            ad = jnp.zeros((SUB_BLOCK, SUB_BLOCK), f32)
            for s in range(SUB_BLOCK):
                row = ck * C + r0 + s
                bs = b_sc[row:row + 1, sl]
                ks = k_sc[row:row + 1, sl]
                e = jnp.exp(jnp.where(trow >= s, bi - bs, -jnp.inf))
                col = jnp.sum(qi * ks * e, axis=-1, keepdims=True)
                ad = jnp.where(lane == s, col, ad)
            diag.append(ad.astype(bf16))
            if i > 0:
                ref = bi[0:1, :]
                q_rel = (qi * jnp.exp(bi - ref)).astype(bf16)
                k_rel = (kh[:r0] * jnp.exp(ref - bh[:r0])).astype(bf16)
                off.append(_dot_nt(q_rel, k_rel).astype(bf16))
        bhs.append(bh)
        qhs.append(qh)
        khs.append(kh)
        vhs.append(v_ref[tok, h * GLA_DV:(h + 1) * GLA_DV])
        a_diag.append(diag)
        a_off.append(off)

    o_intra = []
    for u in range(len(units)):
        rows = []
        for i in range(nb):
            r0 = i * SUB_BLOCK
            o_i = _dot(a_diag[u][i], vhs[u][r0:r0 + SUB_BLOCK])
            if i > 0:
                o_i = o_i + _dot(a_off[u][i - 1], vhs[u][:r0])
            rows.append(o_i)
        o_intra.append(jnp.concatenate(rows, axis=0) if nb > 1 else rows[0])

    states = [st_ref[h] for h in range(GLA_HEADS)]
    for u, (ck, h) in enumerate(units):
        bh, st = bhs[u], states[h]
        blast = bh[C - 1:C, :]
        o = o_intra[u] + _dot_nt((qhs[u] * jnp.exp(bh)).astype(bf16), st.astype(bf16))
        k_dec = (khs[u] * jnp.exp(blast - bh)).astype(bf16)
        states[h] = st * jnp.exp(blast) + _dot_tn(vhs[u], k_dec)
        tok = slice(ck * C, (ck + 1) * C)
        vs = slice(h * GLA_DV, (h + 1) * GLA_DV)
        r = r_ref[tok, vs].astype(f32)
        o_ref[tok, vs] = (_rmsnorm(o, og_ref[...]) * (r * _sigmoid(r))).astype(bf16)
    for h in range(GLA_HEADS):
        st_ref[h] = states[h]

    @pl.when(c == pl.num_programs(1) - 1)
    def _():
        sout_ref[...] = st_ref[...]


def _gla_mix(main, small, w_alpha2, b_alpha, o_gain, s0, C):
    nb, s, _ = main.shape
    nc = min(GLA_STEP_CHUNKS, s // C)
    t = nc * C
    blk = lambda w, col: pl.BlockSpec((None, t, w), lambda b, c: (b, c, col))
    state = (GLA_HEADS, GLA_DV, GLA_DK)
    return pl.pallas_call(
        functools.partial(_gla_kernel, C=C, nc=nc),
        grid=(nb, s // t),
        in_specs=[blk(GLA_QK, 0), blk(GLA_QK, 1), blk(GLA_V, 1), blk(GLA_V, 2),
                  pl.BlockSpec((None, t, LANES), lambda b, c: (b, c, 0)),
                  _resident((LANES, GLA_QK)), _resident((1, GLA_QK)), _resident((1, GLA_DV)),
                  pl.BlockSpec((None,) + state, lambda b, c: (0, 0, 0, 0))],
        out_specs=[blk(GLA_V, 0), pl.BlockSpec((None,) + state, lambda b, c: (b, 0, 0, 0))],
        out_shape=[jax.ShapeDtypeStruct((nb, s, GLA_V), bf16),
                   jax.ShapeDtypeStruct((nb,) + state, f32)],
        scratch_shapes=[pltpu.VMEM(state, f32), pltpu.VMEM((t, GLA_QK), f32),
                        pltpu.VMEM((t, GLA_QK), f32)],
        compiler_params=_params("parallel", "arbitrary"),
        name="gla_mix",
    )(main, main, main, main, small, w_alpha2, b_alpha, o_gain, s0)


def _unit_lower_inverses(lowers, n, span):
    r = lax.broadcasted_iota(jnp.int32, (n, n), 0)
    c = lax.broadcasted_iota(jnp.int32, (n, n), 1)
    eye = (r == c).astype(f32)
    same_block = (r // SUB_BLOCK) == (c // SUB_BLOCK)
    ms = [jnp.where(same_block, -lower, 0.0) for lower in lowers]
    ps = [eye + m for m in ms]
    for _ in range(3):
        ms = [_bdot(m, m) for m in ms]
        ps = [p + _bdot(p, m) for p, m in zip(ps, ms)]
    if span == SUB_BLOCK:
        return ps
    assert span // SUB_BLOCK <= 4
    es = [_bdot(p, jnp.where(same_block, 0.0, lower)) for p, lower in zip(ps, lowers)]
    e2s = [_bdot(e, e) for e in es]
    xs = [eye - e + e2 - _bdot(e, e2) for e, e2 in zip(es, e2s)]
    return [_bdot(x, p) for x, p in zip(xs, ps)]


def _gdn_kernel(x_ref, gate_ref, sm_ref, cw_ref, alog_ref, dtb_ref, og_ref, tail0_ref, s0_ref,
                o_ref, sout_ref, xbuf, s_ref, *, C, nc):
    c = pl.program_id(1)
    T = nc * C

    @pl.when(c == 0)
    def _():
        xbuf[0:8, :] = tail0_ref[...]
        s_ref[...] = s0_ref[...]

    xbuf[8:8 + T, :] = x_ref[...].astype(f32)

    def conv_silu(col):
        sl = slice(col, col + LANES)
        y = cw_ref[0:1, sl] * xbuf[5:5 + T, sl]
        for j in range(1, GDN_CONV):
            y = y + cw_ref[j:j + 1, sl] * xbuf[5 + j:5 + j + T, sl]
        return y * _sigmoid(y)

    def l2norm(x):
        return x * lax.rsqrt(jnp.sum(x * x, axis=-1, keepdims=True) + NORM_EPS)

    sm = sm_ref[...]
    g_all = -jnp.exp(alog_ref[...]) * _softplus(sm + dtb_ref[...])
    beta_all = _sigmoid(sm)
    tr = lax.broadcasted_iota(jnp.int32, (T, T), 0)
    tc = lax.broadcasted_iota(jnp.int32, (T, T), 1)
    in_chunk_tril = (((tr // C) == (tc // C)) & (tr >= tc)).astype(f32)
    b_all = _dot(in_chunk_tril, g_all, _HI)
    hd = GDN_HEADS * GDN_DK
    n = GDN_GROUP * C

    rr = lax.broadcasted_iota(jnp.int32, (n, n), 0)
    cc = lax.broadcasted_iota(jnp.int32, (n, n), 1)
    strict = ((rr // C) == (cc // C)) & (rr > cc)
    eye = (rr == cc).astype(f32)

    groups = [range(g * GDN_GROUP, (g + 1) * GDN_GROUP) for g in range(GDN_HEADS // GDN_GROUP)]
    units = [(ck, g) for ck in range(nc) for g in range(len(groups))]
    rows = [slice(i * C, (i + 1) * C) for i in range(GDN_GROUP)]

    def stack(per_head):
        return [jnp.concatenate([per_head[h][ck * C:(ck + 1) * C] for h in groups[g]], axis=0)
                for ck, g in units]

    every = range(GDN_HEADS)
    q_st = stack([l2norm(conv_silu(h * GDN_DK)) * (GDN_DK ** -0.5) for h in every])
    k_st = stack([l2norm(conv_silu(hd + h * GDN_DK)) for h in every])
    v_st = stack([conv_silu(2 * hd + h * GDN_DV) for h in every])
    b_st = stack([b_all[:, h:h + 1] for h in every])
    beta_st = stack([beta_all[:, GDN_HEADS + h:GDN_HEADS + h + 1] for h in every])
    b_row = [jnp.broadcast_to(b, (n, LANES)).T[0:1, :] for b in b_st]

    dec = [jnp.exp(jnp.where(strict, b - br, -jnp.inf)) for b, br in zip(b_st, b_row)]
    kb_st = [k * beta for k, beta in zip(k_st, beta_st)]
    k16 = [k.astype(bf16) for k in k_st]
    lower = [_dot_nt(kb.astype(bf16), k) * d for kb, k, d in zip(kb_st, k16, dec)]
    attn = [(_dot_nt(q.astype(bf16), k) * (d + eye)).astype(bf16) for q, k, d in zip(q_st, k16, dec)]
    t_inv = _unit_lower_inverses(lower, n, C)
    e_b = [jnp.exp(b) for b in b_st]
    sol = [_bdot(t, jnp.concatenate([v * beta, kb * e], axis=1))
           for t, v, beta, kb, e in zip(t_inv, v_st, beta_st, kb_st, e_b)]
    qe_st = [(q * e).astype(bf16) for q, e in zip(q_st, e_b)]

    states = [s_ref[h] for h in every]
    for u, (ck, g) in enumerate(units):
        heads = groups[g]
        s16 = [states[h].astype(bf16) for h in heads]
        v_new = jnp.concatenate([sol[u][rows[i], :GDN_DV] - _dot(sol[u][rows[i], GDN_DV:].astype(bf16), s16[i])
                                 for i in range(GDN_GROUP)], axis=0)
        vn16 = v_new.astype(bf16)
        o_u = _dot(attn[u], vn16) + jnp.concatenate(
            [_dot(qe_st[u][rows[i]], s16[i]) for i in range(GDN_GROUP)], axis=0)
        for i, h in enumerate(heads):
            b_h = b_st[u][rows[i]]
            b_last = b_h[C - 1:C, :]
            k_dec = (k_st[u][rows[i]] * jnp.exp(b_last - b_h)).astype(bf16)
            states[h] = states[h] * jnp.exp(b_last) + _dot_tn(k_dec, vn16[rows[i]])
            vs = slice(h * GDN_DV, (h + 1) * GDN_DV)
            tok = slice(ck * C, (ck + 1) * C)
            gt = gate_ref[tok, vs].astype(f32)
            o_ref[tok, vs] = (_rmsnorm(o_u[rows[i]], og_ref[...]) * (gt * _sigmoid(gt))).astype(bf16)
    for h in every:
        s_ref[h] = states[h]

    xbuf[0:8, :] = xbuf[T:T + 8, :]

    @pl.when(c == pl.num_programs(1) - 1)
    def _():
        sout_ref[...] = s_ref[...]


def _gdn_mix(main, small, conv_w, a_log, dt_bias, o_gain, tail0, s0, C):
    nb, s, _ = main.shape
    state = (GDN_HEADS, GDN_DK, GDN_DV)
    nc = min(GDN_STEP_CHUNKS, s // C)
    t = nc * C
    return pl.pallas_call(
        functools.partial(_gdn_kernel, C=C, nc=nc),
        grid=(nb, s // t),
        in_specs=[pl.BlockSpec((None, t, GDN_QKV), lambda b, c: (b, c, 0)),
                  pl.BlockSpec((None, t, D_MODEL), lambda b, c: (b, c, GDN_QKV // D_MODEL)),
                  pl.BlockSpec((None, t, LANES), lambda b, c: (b, c, 0)),
                  _resident((GDN_CONV, GDN_QKV)), _resident((1, LANES)), _resident((1, LANES)),
                  _resident((1, GDN_DV)), _resident((8, GDN_QKV)),
                  pl.BlockSpec((None,) + state, lambda b, c: (0, 0, 0, 0))],
        out_specs=[pl.BlockSpec((None, t, D_MODEL), lambda b, c: (b, c, 0)),
                   pl.BlockSpec((None,) + state, lambda b, c: (b, 0, 0, 0))],
        out_shape=[jax.ShapeDtypeStruct((nb, s, D_MODEL), bf16),
                   jax.ShapeDtypeStruct((nb,) + state, f32)],
        scratch_shapes=[pltpu.VMEM((8 + t, GDN_QKV), f32), pltpu.VMEM(state, f32)],
        compiler_params=_params("parallel", "arbitrary"),
        name="gdn_mix",
    )(main, main, small, conv_w, a_log, dt_bias, o_gain, tail0, s0)


def _split_in_proj(w_in, n_main):
    w_small = jnp.pad(w_in[:, n_main:], ((0, 0), (0, LANES - (w_in.shape[1] - n_main))))
    return w_in[:, :n_main].astype(bf16), w_small.astype(bf16)


def _lane_row(v, width=LANES, offset=0):
    return jnp.pad(v.astype(f32), (offset, width - offset - v.shape[0])).reshape(1, width)


def kernel(x, meta_tokens, norm_mix, norm_ffn, w_gate_up, w_down, fox_w_in, fox_b_f, fox_q_gain, fox_k_gain, fox_w_out, gla_w_in, gla_w_alpha2, gla_b_alpha, gla_o_gain, gla_w_out, gdn_w_in, gdn_conv_w, gdn_a_log, gdn_dt_bias, gdn_o_gain, gdn_w_out):
    nb, seq, _ = x.shape
    depth = norm_mix.shape[0]
    hr = x.reshape(nb * seq, D_MODEL)
    hm = meta_tokens.astype(x.dtype)
    fox_tables = _fox_bias_tables()

    for i in range(depth):
        kind, j = i % 3, i // 3
        if kind == 0:
            w_main, w_small = _split_in_proj(fox_w_in[j], 4 * D_MODEL)
            w_out = fox_w_out[j]
            main_m, small_m = _norm_proj(hm, norm_mix[i], w_main, w_small)
            main_r, small_r = _norm_proj(hr, norm_mix[i], w_main, w_small)
            main_m = main_m.reshape(1, N_META, -1)
            main_r = main_r.reshape(nb, seq, -1)
            b_f = _lane_row(fox_b_f[j])
            qg = jnp.tile(fox_q_gain[j], 2).reshape(1, LANES) * (FOX_HEAD_DIM ** -0.5 * LOG2E)
            kg = jnp.tile(fox_k_gain[j], 2).reshape(1, LANES)
            qx_m, kx_m, vx_m, c_m = _fox_prep(main_m, small_m.reshape(1, N_META, LANES), b_f, qg, kg,
                                              jnp.zeros((1, LANES), f32), fox_tables, N_META)
            qx_r, kx_r, vx_r, c_r = _fox_prep(main_r, small_r.reshape(nb, seq, LANES), b_f, qg, kg,
                                              c_m[0, 0, 1:2, :], fox_tables, min(FOX_BLOCK, seq))
            groups = FOX_HEADS // FOX_STEP_HEADS
            o_m = _fox_attn(qx_m, kx_m, vx_m, kx_m, vx_m, main_m, jnp.zeros((1, groups, 1), jnp.int32), True)
            o_r = _fox_attn(qx_r, kx_r, vx_r, kx_m, vx_m, main_r,
                            _fox_first_block(c_r, fox_q_gain[j], fox_k_gain[j]), False)
        elif kind == 1:
            w_main, w_small = _split_in_proj(gla_w_in[j], 2 * GLA_QK + 2 * GLA_V)
            w_out = gla_w_out[j]
            main_m, small_m = _norm_proj(hm, norm_mix[i], w_main, w_small)
            main_r, small_r = _norm_proj(hr, norm_mix[i], w_main, w_small)
            w2 = jnp.pad(gla_w_alpha2[j], ((0, LANES - gla_w_alpha2.shape[1]), (0, 0)))
            ba = gla_b_alpha[j].reshape(1, GLA_QK)
            og = gla_o_gain[j].reshape(1, GLA_DV)
            s0 = jnp.zeros((1, GLA_HEADS, GLA_DV, GLA_DK), f32)
            o_m, s_m = _gla_mix(main_m.reshape(1, N_META, -1), small_m.reshape(1, N_META, LANES),
                                w2, ba, og, s0, N_META)
            o_r, _ = _gla_mix(main_r.reshape(nb, seq, -1), small_r.reshape(nb, seq, LANES),
                              w2, ba, og, s_m, GLA_CHUNK)
        else:
            w_main, w_small = _split_in_proj(gdn_w_in[j], GDN_QKV + GDN_HEADS * GDN_DV)
            w_out = gdn_w_out[j]
            main_m, small_m = _norm_proj(hm, norm_mix[i], w_main, w_small)
            main_r, small_r = _norm_proj(hr, norm_mix[i], w_main, w_small)
            cw = gdn_conv_w[j].reshape(GDN_CONV, GDN_QKV)
            alog = _lane_row(gdn_a_log[j])
            dtb = _lane_row(gdn_dt_bias[j])
            og = gdn_o_gain[j].reshape(1, GDN_DV)
            s0 = jnp.zeros((1, GDN_HEADS, GDN_DK, GDN_DV), f32)
            o_m, s_m = _gdn_mix(main_m.reshape(1, N_META, -1), small_m.reshape(1, N_META, LANES),
                                cw, alog, dtb, og, jnp.zeros((8, GDN_QKV), f32), s0, N_META)
            tail = main_m[N_META - 8:, :GDN_QKV].astype(f32)
            o_r, _ = _gdn_mix(main_r.reshape(nb, seq, -1), small_r.reshape(nb, seq, LANES),
                              cw, alog, dtb, og, tail, s_m, GDN_CHUNK)

        wo = w_out.astype(bf16)
        wgu = w_gate_up[i].astype(bf16)
        wd = w_down[i].astype(bf16)
        hm = _mix_ffn(hm, o_m.reshape(N_META, D_MODEL), wo, norm_ffn[i], wgu, wd)
        hr = _mix_ffn(hr, o_r.reshape(nb * seq, D_MODEL), wo, norm_ffn[i], wgu, wd)

    return hr.reshape(nb, seq, D_MODEL)
```

```python
import functools
import math

import jax
import jax.numpy as jnp
from jax import lax
from jax.experimental import pallas as pl
from jax.experimental.pallas import tpu as pltpu

f32 = jnp.float32
bf16 = jnp.bfloat16

D_MODEL = 1024
N_META = 16
NORM_EPS = 1e-6
LANES = 128
SUB_BLOCK = 16
LOG2E = math.log2(math.e)

FOX_HEADS = 16
FOX_HEAD_DIM = 64
FOX_BLOCK = 256
FOX_STEP_HEADS = 8
FOX_ZERO_EXP = 160.0
FOX_V_ROWS = FOX_HEAD_DIM + 16
FOX_BIAS_LANE = FOX_HEAD_DIM

GLA_HEADS = 4
GLA_DK = 128
GLA_DV = 256
GLA_QK = GLA_HEADS * GLA_DK
GLA_V = GLA_HEADS * GLA_DV
GLA_GATE_NORMALIZER = 16.0
GLA_CHUNK = 64
GLA_STEP_CHUNKS = 4

GDN_HEADS = 8
GDN_DK = 128
GDN_DV = 128
GDN_CONV = 4
GDN_CHUNK = 64
GDN_QKV = 3 * GDN_HEADS * GDN_DK
GDN_GROUP = 2
GDN_STEP_CHUNKS = 4

D_FF = 2816
FFN_CHUNK = 256

ROW_TILE = 1024
VMEM_LIMIT = 56 * 2 ** 20

_HI = lax.Precision.HIGHEST


def _params(*sem):
    return pltpu.CompilerParams(dimension_semantics=sem, vmem_limit_bytes=VMEM_LIMIT)


def _resident(shape):
    nd = len(shape)
    return pl.BlockSpec(shape, lambda *_: (0,) * nd, pipeline_mode=pl.Buffered(1))


def _dot(a, b, precision=None):
    return jnp.dot(a, b, preferred_element_type=f32, precision=precision)


def _dot_nt(a, b, precision=None):
    return lax.dot_general(a, b, (((1,), (1,)), ((), ())), preferred_element_type=f32,
                           precision=precision)


def _dot_tn(a, b, precision=None):
    return lax.dot_general(a, b, (((0,), (0,)), ((), ())), preferred_element_type=f32,
                           precision=precision)


def _bdot(a, b):
    return _dot(a.astype(bf16), b.astype(bf16))


def _sigmoid(x):
    return 1.0 / (1.0 + jnp.exp(-x))


def _softplus(x):
    return jnp.maximum(x, 0.0) + jnp.log(1.0 + jnp.exp(-jnp.abs(x)))


def _log_sigmoid(x):
    return -_softplus(-x)


def _rmsnorm(x, g):
    return x * lax.rsqrt(jnp.mean(x * x, axis=-1, keepdims=True) + NORM_EPS) * g


def _tril(n, k=0):
    r = lax.broadcasted_iota(jnp.int32, (n, n), 0)
    c = lax.broadcasted_iota(jnp.int32, (n, n), 1)
    return r + k >= c


def _chunk_cumsum(x, chunk):
    pos = lax.broadcasted_iota(jnp.int32, (x.shape[0], 1), 0) % chunk
    d = 1
    while d < chunk:
        x = x + jnp.where(pos >= d, pltpu.roll(x, d, axis=0), 0.0)
        d *= 2
    return x


def _norm_proj_kernel(h_ref, g_ref, w_ref, ws_ref, o_ref, os_ref, *, tn):
    y = _rmsnorm(h_ref[...], g_ref[...]).astype(bf16)
    os_ref[...] = _dot(y, ws_ref[...])
    for j in range(w_ref.shape[1] // tn):
        o_ref[:, j * tn:(j + 1) * tn] = _dot(y, w_ref[:, j * tn:(j + 1) * tn]).astype(bf16)


def _norm_proj(h, gain, w_main, w_small):
    m = h.shape[0]
    tm = min(ROW_TILE, m)
    n = w_main.shape[1]
    return pl.pallas_call(
        functools.partial(_norm_proj_kernel, tn=512),
        grid=(m // tm,),
        in_specs=[pl.BlockSpec((tm, D_MODEL), lambda i: (i, 0)),
                  _resident((1, D_MODEL)), _resident(w_main.shape), _resident(w_small.shape)],
        out_specs=[pl.BlockSpec((tm, n), lambda i: (i, 0)),
                   pl.BlockSpec((tm, LANES), lambda i: (i, 0))],
        out_shape=[jax.ShapeDtypeStruct((m, n), bf16), jax.ShapeDtypeStruct((m, LANES), f32)],
        compiler_params=_params("parallel"),
        name="norm_proj",
    )(h, gain.reshape(1, D_MODEL), w_main, w_small)


def _mix_ffn_kernel(h_ref, o_ref, wo_ref, g_ref, wgu_ref, wd_ref, out_ref):
    hmid = h_ref[...] + _dot(o_ref[...], wo_ref[...])
    y = _rmsnorm(hmid, g_ref[...]).astype(bf16)
    out_ref[...] = hmid
    for f in range(D_FF // FFN_CHUNK):
        lo = f * FFN_CHUNK
        gt = _dot(y, wgu_ref[:, lo:lo + FFN_CHUNK])
        up = _dot(y, wgu_ref[:, D_FF + lo:D_FF + lo + FFN_CHUNK])
        act = (gt * _sigmoid(gt) * up).astype(bf16)
        out_ref[...] += _dot(act, wd_ref[lo:lo + FFN_CHUNK, :])


def _mix_ffn(h, o, w_out, gain, w_gate_up, w_down):
    m = h.shape[0]
    tm = min(ROW_TILE, m)
    row = lambda i: (i, 0)
    return pl.pallas_call(
        _mix_ffn_kernel,
        grid=(m // tm,),
        in_specs=[pl.BlockSpec((tm, D_MODEL), row), pl.BlockSpec((tm, D_MODEL), row),
                  _resident(w_out.shape), _resident((1, D_MODEL)),
                  _resident(w_gate_up.shape), _resident(w_down.shape)],
        out_specs=pl.BlockSpec((tm, D_MODEL), row),
        out_shape=jax.ShapeDtypeStruct((m, D_MODEL), f32),
        compiler_params=_params("parallel"),
        name="mix_ffn",
    )(h, o, w_out, gain.reshape(1, D_MODEL), w_gate_up, w_down)


def _fox_bias_tables():
    r = jnp.arange(3 * LANES)[:, None]
    lane = jnp.arange(LANES)[None, :]
    piece, src_head = r // LANES, r % LANES
    place_k = ((src_head < FOX_HEADS) & (lane == FOX_BIAS_LANE + 3 * src_head + piece)).astype(bf16)
    head = jnp.arange(FOX_HEADS)[:, None]
    first = FOX_BIAS_LANE + 3 * head
    const_q = -((lane >= first) & (lane < first + 3)).astype(f32)
    r2 = jnp.arange(2 * LANES)[:, None] % LANES
    head_sum = ((r2 // FOX_HEAD_DIM) == (lane // FOX_HEAD_DIM)).astype(bf16)
    return place_k, const_q, head_sum


def _fox_proj_kernel(h_ref, g_ref, w_ref, ws_ref, bf_ref, qg_ref, kg_ref, c0_ref, pk_ref, cq_ref, hs_ref,
                     gate_ref, qx_ref, kx_ref, vx_ref, cedge_ref, carry_ref, ext_ref, qkv_ref,
                     *, tp, tiles_per_seq):
    tm = h_ref.shape[0]

    @pl.when(pl.program_id(0) % tiles_per_seq == 0)
    def _():
        carry_ref[...] = c0_ref[...]

    y = _rmsnorm(h_ref[...], g_ref[...]).astype(bf16)
    f = _dot(y, ws_ref[...])
    tril = _tril(tp).astype(f32)
    for blk in range(tm // tp):
        rows = slice(blk * tp, (blk + 1) * tp)
        log_f = _log_sigmoid(f[rows] + bf_ref[...])
        cum = _dot(tril, log_f, _HI) + carry_ref[...]
        carry_ref[...] = cum[tp - 1:tp, :]
        cedge_ref[blk, 0:1, :] = cum[0:1, :]
        cedge_ref[blk, 1:2, :] = cum[tp - 1:tp, :]
        c2 = cum * LOG2E
        hi = c2.astype(bf16)
        r1 = c2 - hi.astype(f32)
        mid = r1.astype(bf16)
        low = (r1 - mid.astype(f32)).astype(bf16)
        x3 = jnp.concatenate([hi, mid, low], axis=1)
        ext_ref[rows, :] = _dot(x3, pk_ref[...])

    lo = lax.broadcasted_iota(jnp.int32, (tm, LANES), 1) < FOX_HEAD_DIM

    def head_norm(x, g):
        x2 = x * x
        hi2 = x2.astype(bf16)
        lo2 = (x2 - hi2.astype(f32)).astype(bf16)
        ss = _dot(jnp.concatenate([hi2, lo2], axis=1), hs_ref[...])
        return x * lax.rsqrt(ss * (1.0 / FOX_HEAD_DIM) + NORM_EPS) * g

    ones = jnp.ones((FOX_V_ROWS - FOX_HEAD_DIM, tp), bf16)
    wide = 2 * LANES
    n_chunks = D_MODEL // wide

    def project(c):
        for part in range(3):
            qkv_ref[c % 2, part] = _dot(y, w_ref[:, part * D_MODEL + c * wide:part * D_MODEL + (c + 1) * wide])
        gate_ref[:, c * wide:(c + 1) * wide] = _dot(
            y, w_ref[:, 3 * D_MODEL + c * wide:3 * D_MODEL + (c + 1) * wide]).astype(bf16)

    project(0)
    for c in range(n_chunks):
        if c + 1 < n_chunks:
            project(c + 1)
        for pp in range(2):
            sl = slice(pp * LANES, (pp + 1) * LANES)
            qn = head_norm(qkv_ref[c % 2, 0, :, sl], qg_ref[...])
            kn = head_norm(qkv_ref[c % 2, 1, :, sl], kg_ref[...])
            halves = ((qn, kn), (pltpu.roll(qn, FOX_HEAD_DIM, axis=1), pltpu.roll(kn, FOX_HEAD_DIM, axis=1)))
            for hh, (qv, kv) in enumerate(halves):
                h = 2 * (2 * c + pp) + hh
                qx_ref[h] = jnp.where(lo, qv, cq_ref[h:h + 1, :]).astype(bf16)
                kx_ref[h] = jnp.where(lo, kv, ext_ref[...]).astype(bf16)
            for blk in range(tm // tp):
                vt = qkv_ref[c % 2, 2, blk * tp:(blk + 1) * tp, sl].T
                for hh in range(2):
                    h = 2 * (2 * c + pp) + hh
                    vx_ref[h, blk, 0:FOX_HEAD_DIM, :] = vt[hh * FOX_HEAD_DIM:(hh + 1) * FOX_HEAD_DIM].astype(bf16)
                    vx_ref[h, blk, FOX_HEAD_DIM:FOX_V_ROWS, :] = ones


def _fox_proj(h, nb, gain, w_main, w_small, b_f, q_gain, k_gain, c0, tables):
    m = h.shape[0]
    s = m // nb
    tm = min(ROW_TILE, s)
    tp = min(FOX_BLOCK, s)
    tps = s // tm
    head_blk = pl.BlockSpec((None, FOX_HEADS, tm, LANES), lambda i: (i // tps, 0, i % tps, 0))
    return pl.pallas_call(
        functools.partial(_fox_proj_kernel, tp=tp, tiles_per_seq=tps),
        grid=(m // tm,),
        in_specs=[pl.BlockSpec((tm, D_MODEL), lambda i: (i, 0)), _resident((1, D_MODEL)),
                  _resident(w_main.shape), _resident(w_small.shape),
                  _resident((1, LANES)), _resident((1, LANES)), _resident((1, LANES)),
                  _resident((1, LANES))] + [_resident(t.shape) for t in tables],
        out_specs=[pl.BlockSpec((tm, D_MODEL), lambda i: (i, 0)), head_blk, head_blk,
                   pl.BlockSpec((None, FOX_HEADS, tm // tp, FOX_V_ROWS, tp), lambda i: (i // tps, 0, i % tps, 0, 0)),
                   pl.BlockSpec((None, tm // tp, 2, LANES), lambda i: (i // tps, i % tps, 0, 0))],
        out_shape=[jax.ShapeDtypeStruct((m, D_MODEL), bf16),
                   jax.ShapeDtypeStruct((nb, FOX_HEADS, s, LANES), bf16),
                   jax.ShapeDtypeStruct((nb, FOX_HEADS, s, LANES), bf16),
                   jax.ShapeDtypeStruct((nb, FOX_HEADS, s // tp, FOX_V_ROWS, tp), bf16),
                   jax.ShapeDtypeStruct((nb, s // tp, 2, LANES), f32)],
        scratch_shapes=[pltpu.VMEM((1, LANES), f32), pltpu.VMEM((tm, LANES), f32),
                        pltpu.VMEM((2, 3, tm, 2 * LANES), f32)],
        compiler_params=_params("arbitrary"),
        name="fox_proj",
    )(h, gain.reshape(1, D_MODEL), w_main, w_small, b_f, q_gain, k_gain, c0, *tables)


def _fox_attn_kernel(start_ref, qx_ref, kx_ref, vx_ref, kxm_ref, vxm_ref, gate_ref, o_ref, m_ref, acc_ref,
                     s_ref, mx_ref, *, bq, nh, meta_only):
    qi = pl.program_id(2)
    heads = range(nh)
    visible = (lax.broadcasted_iota(jnp.int32, (bq, bq), 0)
               <= lax.broadcasted_iota(jnp.int32, (bq, bq), 1))

    def scores(j, diagonal):
        start = pl.multiple_of(j * bq, bq)
        ss = [_dot_nt(kx_ref[h, pl.ds(start, bq), :], qx_ref[h]) for h in heads]
        if diagonal:
            ss = [jnp.where(visible, s, -jnp.inf) for s in ss]
        return tuple(ss), tuple(jnp.max(s, axis=0, keepdims=True) for s in ss)

    def stage(blk):
        ss, mx = blk
        for h in heads:
            s_ref[h] = ss[h]
            mx_ref[h] = mx[h]

    sm = [_dot_nt(kxm_ref[h], qx_ref[h]) for h in heads]
    if meta_only:
        sm = [jnp.where(visible, s, -jnp.inf) for s in sm]

    if not meta_only:
        stage(scores(qi, True))

    ms = [jnp.max(s, axis=0, keepdims=True) for s in sm]
    ps = [jnp.exp2(s - m).astype(bf16) for s, m in zip(sm, ms)]
    for h in heads:
        m_ref[h] = ms[h]
        acc_ref[h] = _dot(vxm_ref[h], ps[h])

    if not meta_only:
        def consume(j):
            m_old = [m_ref[h] for h in heads]
            m_new = [jnp.maximum(m_old[h], mx_ref[h]) for h in heads]
            p = [jnp.exp2(s_ref[h] - m_new[h]).astype(bf16) for h in heads]
            pv = [_dot(vx_ref[h, j], p[h]) for h in heads]
            for h in heads:
                acc_ref[h] = jnp.exp2(m_old[h] - m_new[h]) * acc_ref[h] + pv[h]
                m_ref[h] = m_new[h]

        def body(j, j_prev):
            nxt = scores(j, False)
            consume(j_prev)
            stage(nxt)
            return j

        first = start_ref[pl.program_id(0), pl.program_id(1), qi]
        consume(lax.fori_loop(first, qi, body, qi))

    for pair in range(nh // 2):
        outs = []
        for h in (2 * pair, 2 * pair + 1):
            a = acc_ref[h]
            outs.append(a[0:FOX_HEAD_DIM] / a[FOX_HEAD_DIM:FOX_HEAD_DIM + 1])
        o = jnp.concatenate(outs, axis=0).T
        cols = slice(pair * LANES, (pair + 1) * LANES)
        o_ref[:, cols] = (o * _sigmoid(gate_ref[:, cols].astype(f32))).astype(bf16)


def _fox_first_block(cedge, q_gain, k_gain):
    smax = 8.0 * jnp.max(jnp.abs(q_gain)) * jnp.max(jnp.abs(k_gain)) * (LOG2E * 1.02)
    c2 = cedge[..., :FOX_HEADS] * LOG2E
    gap = c2[:, :, None, 0, :] - c2[:, None, :, 1, :]
    needed = (2.0 * smax + gap) > -FOX_ZERO_EXP
    nb, nq, nk, _ = needed.shape
    needed = needed.reshape(nb, nq, nk, FOX_HEADS // FOX_STEP_HEADS, FOX_STEP_HEADS).any(-1)
    needed = needed | (jnp.arange(nk)[None, None, :, None] >= jnp.arange(nq)[None, :, None, None])
    return jnp.argmax(needed, axis=2).astype(jnp.int32).transpose(0, 2, 1)


def _fox_attn(qx, kx, vx, kx_meta, vx_meta, gate, first_block, meta_only):
    nb, _, s, _ = qx.shape
    bq = min(FOX_BLOCK, s)
    nh = FOX_STEP_HEADS
    w = nh // 2 * LANES
    return pl.pallas_call(
        functools.partial(_fox_attn_kernel, bq=bq, nh=nh, meta_only=meta_only),
        grid=(nb, FOX_HEADS // nh, s // bq),
        in_specs=[pl.BlockSpec(memory_space=pltpu.SMEM),
                  pl.BlockSpec((None, nh, bq, LANES), lambda b, p, i: (b, p, i, 0)),
                  pl.BlockSpec((None, nh, s, LANES), lambda b, p, i: (b, p, 0, 0)),
                  pl.BlockSpec((None, nh, s // bq, FOX_V_ROWS, bq), lambda b, p, i: (b, p, 0, 0, 0)),
                  pl.BlockSpec((None, nh, N_META, LANES), lambda b, p, i: (0, p, 0, 0)),
                  pl.BlockSpec((None, nh, None, FOX_V_ROWS, N_META), lambda b, p, i: (0, p, 0, 0, 0)),
                  pl.BlockSpec((None, bq, w), lambda b, p, i: (b, i, p))],
        out_specs=pl.BlockSpec((None, bq, w), lambda b, p, i: (b, i, p)),
        out_shape=jax.ShapeDtypeStruct((nb, s, D_MODEL), bf16),
        scratch_shapes=[pltpu.VMEM((nh, 1, bq), f32), pltpu.VMEM((nh, FOX_V_ROWS, bq), f32),
                        pltpu.VMEM((nh, bq, bq), f32), pltpu.VMEM((nh, 1, bq), f32)],
        compiler_params=_params("parallel", "parallel", "arbitrary"),
        name="fox_attn",
    )(first_block, qx, kx, vx, kx_meta, vx_meta, gate)


def _gla_kernel(q_ref, k_ref, v_ref, r_ref, a_ref, w2_ref, ba_ref, og_ref, s0_ref,
                o_ref, sout_ref, st_ref, b_sc, k_sc, *, C, nc):
    c = pl.program_id(1)
    T = nc * C

    @pl.when(c == 0)
    def _():
        st_ref[...] = s0_ref[...]

    g = _log_sigmoid(_dot(a_ref[...], w2_ref[...], _HI) + ba_ref[...]) * (1.0 / GLA_GATE_NORMALIZER)
    b_sc[...] = _chunk_cumsum(g, C)
    k_sc[...] = k_ref[...].astype(f32)

    nb = C // SUB_BLOCK
    trow = lax.broadcasted_iota(jnp.int32, (SUB_BLOCK, 1), 0)
    lane = lax.broadcasted_iota(jnp.int32, (SUB_BLOCK, SUB_BLOCK), 1)

    units = [(ck, h) for ck in range(nc) for h in range(GLA_HEADS)]
    bhs, qhs, khs, vhs, a_diag, a_off = [], [], [], [], [], []
    for ck, h in units:
        tok = slice(ck * C, (ck + 1) * C)
        sl = slice(h * GLA_DK, (h + 1) * GLA_DK)
        bh = b_sc[tok, sl]
        qh = q_ref[tok, sl].astype(f32) * (GLA_DK ** -0.5)
        kh = k_sc[tok, sl]
        diag, off = [], []
        for i in range(nb):
            r0 = i * SUB_BLOCK
            bi = bh[r0:r0 + SUB_BLOCK]
            qi = qh[r0:r0 + SUB_BLOCK]
            ad = jnp.zeros((SUB_BLOCK, SUB_BLOCK), f32)
            for s in range(SUB_BLOCK):
                row = ck * C + r0 + s
                bs = b_sc[row:row + 1, sl]
                ks = k_sc[row:row + 1, sl]
                e = jnp.exp(jnp.where(trow >= s, bi - bs, -jnp.inf))
                col = jnp.sum(qi * ks * e, axis=-1, keepdims=True)
                ad = jnp.where(lane == s, col, ad)
            diag.append(ad.astype(bf16))
            if i > 0:
                ref = bi[0:1, :]
                q_rel = (qi * jnp.exp(bi - ref)).astype(bf16)
                k_rel = (kh[:r0] * jnp.exp(ref - bh[:r0])).astype(bf16)
                off.append(_dot_nt(q_rel, k_rel).astype(bf16))
        bhs.append(bh)
        qhs.append(qh)
        khs.append(kh)
        vhs.append(v_ref[tok, h * GLA_DV:(h + 1) * GLA_DV])
        a_diag.append(diag)
        a_off.append(off)

    o_intra = []
    for u in range(len(units)):
        rows = []
        for i in range(nb):
            r0 = i * SUB_BLOCK
            o_i = _dot(a_diag[u][i], vhs[u][r0:r0 + SUB_BLOCK])
            if i > 0:
                o_i = o_i + _dot(a_off[u][i - 1], vhs[u][:r0])
            rows.append(o_i)
        o_intra.append(jnp.concatenate(rows, axis=0) if nb > 1 else rows[0])

    states = [st_ref[h] for h in range(GLA_HEADS)]
    for u, (ck, h) in enumerate(units):
        bh, st = bhs[u], states[h]
        blast = bh[C - 1:C, :]
        o = o_intra[u] + _dot_nt((qhs[u] * jnp.exp(bh)).astype(bf16), st.astype(bf16))
        k_dec = (khs[u] * jnp.exp(blast - bh)).astype(bf16)
        states[h] = st * jnp.exp(blast) + _dot_tn(vhs[u], k_dec)
        tok = slice(ck * C, (ck + 1) * C)
        vs = slice(h * GLA_DV, (h + 1) * GLA_DV)
        r = r_ref[tok, vs].astype(f32)
        o_ref[tok, vs] = (_rmsnorm(o, og_ref[...]) * (r * _sigmoid(r))).astype(bf16)
    for h in range(GLA_HEADS):
        st_ref[h] = states[h]

    @pl.when(c == pl.num_programs(1) - 1)
    def _():
        sout_ref[...] = st_ref[...]


def _gla_mix(main, small, w_alpha2, b_alpha, o_gain, s0, C):
    nb, s, _ = main.shape
    nc = min(GLA_STEP_CHUNKS, s // C)
    t = nc * C
    blk = lambda w, col: pl.BlockSpec((None, t, w), lambda b, c: (b, c, col))
    state = (GLA_HEADS, GLA_DV, GLA_DK)
    return pl.pallas_call(
        functools.partial(_gla_kernel, C=C, nc=nc),
        grid=(nb, s // t),
        in_specs=[blk(GLA_QK, 0), blk(GLA_QK, 1), blk(GLA_V, 1), blk(GLA_V, 2),
                  pl.BlockSpec((None, t, LANES), lambda b, c: (b, c, 0)),
                  _resident((LANES, GLA_QK)), _resident((1, GLA_QK)), _resident((1, GLA_DV)),
                  pl.BlockSpec((None,) + state, lambda b, c: (0, 0, 0, 0))],
        out_specs=[blk(GLA_V, 0), pl.BlockSpec((None,) + state, lambda b, c: (b, 0, 0, 0))],
        out_shape=[jax.ShapeDtypeStruct((nb, s, GLA_V), bf16),
                   jax.ShapeDtypeStruct((nb,) + state, f32)],
        scratch_shapes=[pltpu.VMEM(state, f32), pltpu.VMEM((t, GLA_QK), f32),
                        pltpu.VMEM((t, GLA_QK), f32)],
        compiler_params=_params("parallel", "arbitrary"),
        name="gla_mix",
    )(main, main, main, main, small, w_alpha2, b_alpha, o_gain, s0)


def _unit_lower_inverses(lowers, n, span):
    r = lax.broadcasted_iota(jnp.int32, (n, n), 0)
    c = lax.broadcasted_iota(jnp.int32, (n, n), 1)
    eye = (r == c).astype(f32)
    same_block = (r // SUB_BLOCK) == (c // SUB_BLOCK)
    ms = [jnp.where(same_block, -lower, 0.0) for lower in lowers]
    ps = [eye + m for m in ms]
    for _ in range(3):
        ms = [_bdot(m, m) for m in ms]
        ps = [p + _bdot(p, m) for p, m in zip(ps, ms)]
    if span == SUB_BLOCK:
        return ps
    assert span // SUB_BLOCK <= 4
    es = [_bdot(p, jnp.where(same_block, 0.0, lower)) for p, lower in zip(ps, lowers)]
    e2s = [_bdot(e, e) for e in es]
    xs = [eye - e + e2 - _bdot(e, e2) for e, e2 in zip(es, e2s)]
    return [_bdot(x, p) for x, p in zip(xs, ps)]


def _gdn_kernel(x_ref, gate_ref, sm_ref, cw_ref, alog_ref, dtb_ref, og_ref, tail0_ref, s0_ref,
                o_ref, sout_ref, xbuf, s_ref, *, C, nc):
    c = pl.program_id(1)
    T = nc * C

    @pl.when(c == 0)
    def _():
        xbuf[0:8, :] = tail0_ref[...]
        s_ref[...] = s0_ref[...]

    xbuf[8:8 + T, :] = x_ref[...].astype(f32)

    def conv_silu(col):
        sl = slice(col, col + LANES)
        y = cw_ref[0:1, sl] * xbuf[5:5 + T, sl]
        for j in range(1, GDN_CONV):
            y = y + cw_ref[j:j + 1, sl] * xbuf[5 + j:5 + j + T, sl]
        return y * _sigmoid(y)

    def l2norm(x):
        return x * lax.rsqrt(jnp.sum(x * x, axis=-1, keepdims=True) + NORM_EPS)

    sm = sm_ref[...]
    g_all = -jnp.exp(alog_ref[...]) * _softplus(sm + dtb_ref[...])
    beta_all = _sigmoid(sm)
    tr = lax.broadcasted_iota(jnp.int32, (T, T), 0)
    tc = lax.broadcasted_iota(jnp.int32, (T, T), 1)
    in_chunk_tril = (((tr // C) == (tc // C)) & (tr >= tc)).astype(f32)
    b_all = _dot(in_chunk_tril, g_all, _HI)
    hd = GDN_HEADS * GDN_DK
    n = GDN_GROUP * C

    rr = lax.broadcasted_iota(jnp.int32, (n, n), 0)
    cc = lax.broadcasted_iota(jnp.int32, (n, n), 1)
    strict = ((rr // C) == (cc // C)) & (rr > cc)
    eye = (rr == cc).astype(f32)

    groups = [range(g * GDN_GROUP, (g + 1) * GDN_GROUP) for g in range(GDN_HEADS // GDN_GROUP)]
    units = [(ck, g) for ck in range(nc) for g in range(len(groups))]
    rows = [slice(i * C, (i + 1) * C) for i in range(GDN_GROUP)]

    def stack(per_head):
        return [jnp.concatenate([per_head[h][ck * C:(ck + 1) * C] for h in groups[g]], axis=0)
                for ck, g in units]

    every = range(GDN_HEADS)
    q_st = stack([l2norm(conv_silu(h * GDN_DK)) * (GDN_DK ** -0.5) for h in every])
    k_st = stack([l2norm(conv_silu(hd + h * GDN_DK)) for h in every])
    v_st = stack([conv_silu(2 * hd + h * GDN_DV) for h in every])
    b_st = stack([b_all[:, h:h + 1] for h in every])
    beta_st = stack([beta_all[:, GDN_HEADS + h:GDN_HEADS + h + 1] for h in every])
    b_row = [jnp.broadcast_to(b, (n, LANES)).T[0:1, :] for b in b_st]

    dec = [jnp.exp(jnp.where(strict, b - br, -jnp.inf)) for b, br in zip(b_st, b_row)]
    kb_st = [k * beta for k, beta in zip(k_st, beta_st)]
    k16 = [k.astype(bf16) for k in k_st]
    lower = [_dot_nt(kb.astype(bf16), k) * d for kb, k, d in zip(kb_st, k16, dec)]
    attn = [(_dot_nt(q.astype(bf16), k) * (d + eye)).astype(bf16) for q, k, d in zip(q_st, k16, dec)]
    t_inv = _unit_lower_inverses(lower, n, C)
    e_b = [jnp.exp(b) for b in b_st]
    sol = [_bdot(t, jnp.concatenate([v * beta, kb * e], axis=1))
           for t, v, beta, kb, e in zip(t_inv, v_st, beta_st, kb_st, e_b)]
    qe_st = [(q * e).astype(bf16) for q, e in zip(q_st, e_b)]

    states = [s_ref[h] for h in every]
    for u, (ck, g) in enumerate(units):
        heads = groups[g]
        s16 = [states[h].astype(bf16) for h in heads]
        v_new = jnp.concatenate([sol[u][rows[i], :GDN_DV] - _dot(sol[u][rows[i], GDN_DV:].astype(bf16), s16[i])
                                 for i in range(GDN_GROUP)], axis=0)
        vn16 = v_new.astype(bf16)
        o_u = _dot(attn[u], vn16) + jnp.concatenate(
            [_dot(qe_st[u][rows[i]], s16[i]) for i in range(GDN_GROUP)], axis=0)
        for i, h in enumerate(heads):
            b_h = b_st[u][rows[i]]
            b_last = b_h[C - 1:C, :]
            k_dec = (k_st[u][rows[i]] * jnp.exp(b_last - b_h)).astype(bf16)
            states[h] = states[h] * jnp.exp(b_last) + _dot_tn(k_dec, vn16[rows[i]])
            vs = slice(h * GDN_DV, (h + 1) * GDN_DV)
            tok = slice(ck * C, (ck + 1) * C)
            gt = gate_ref[tok, vs].astype(f32)
            o_ref[tok, vs] = (_rmsnorm(o_u[rows[i]], og_ref[...]) * (gt * _sigmoid(gt))).astype(bf16)
    for h in every:
        s_ref[h] = states[h]

    xbuf[0:8, :] = xbuf[T:T + 8, :]

    @pl.when(c == pl.num_programs(1) - 1)
    def _():
        sout_ref[...] = s_ref[...]


def _gdn_mix(main, small, conv_w, a_log, dt_bias, o_gain, tail0, s0, C):
    nb, s, _ = main.shape
    state = (GDN_HEADS, GDN_DK, GDN_DV)
    nc = min(GDN_STEP_CHUNKS, s // C)
    t = nc * C
    return pl.pallas_call(
        functools.partial(_gdn_kernel, C=C, nc=nc),
        grid=(nb, s // t),
        in_specs=[pl.BlockSpec((None, t, GDN_QKV), lambda b, c: (b, c, 0)),
                  pl.BlockSpec((None, t, D_MODEL), lambda b, c: (b, c, GDN_QKV // D_MODEL)),
                  pl.BlockSpec((None, t, LANES), lambda b, c: (b, c, 0)),
                  _resident((GDN_CONV, GDN_QKV)), _resident((1, LANES)), _resident((1, LANES)),
                  _resident((1, GDN_DV)), _resident((8, GDN_QKV)),
                  pl.BlockSpec((None,) + state, lambda b, c: (0, 0, 0, 0))],
        out_specs=[pl.BlockSpec((None, t, D_MODEL), lambda b, c: (b, c, 0)),
                   pl.BlockSpec((None,) + state, lambda b, c: (b, 0, 0, 0))],
        out_shape=[jax.ShapeDtypeStruct((nb, s, D_MODEL), bf16),
                   jax.ShapeDtypeStruct((nb,) + state, f32)],
        scratch_shapes=[pltpu.VMEM((8 + t, GDN_QKV), f32), pltpu.VMEM(state, f32)],
        compiler_params=_params("parallel", "arbitrary"),
        name="gdn_mix",
    )(main, main, small, conv_w, a_log, dt_bias, o_gain, tail0, s0)


def _split_in_proj(w_in, n_main):
    w_small = jnp.pad(w_in[:, n_main:], ((0, 0), (0, LANES - (w_in.shape[1] - n_main))))
    return w_in[:, :n_main].astype(bf16), w_small.astype(bf16)


def _lane_row(v, width=LANES, offset=0):
    return jnp.pad(v.astype(f32), (offset, width - offset - v.shape[0])).reshape(1, width)


def kernel(x, meta_tokens, norm_mix, norm_ffn, w_gate_up, w_down, fox_w_in, fox_b_f, fox_q_gain, fox_k_gain, fox_w_out, gla_w_in, gla_w_alpha2, gla_b_alpha, gla_o_gain, gla_w_out, gdn_w_in, gdn_conv_w, gdn_a_log, gdn_dt_bias, gdn_o_gain, gdn_w_out):
    nb, seq, _ = x.shape
    depth = norm_mix.shape[0]
    hr = x.reshape(nb * seq, D_MODEL)
    hm = meta_tokens.astype(x.dtype)
    fox_tables = _fox_bias_tables()

    for i in range(depth):
        kind, j = i % 3, i // 3
        if kind == 0:
            w_main, w_small = _split_in_proj(fox_w_in[j], 4 * D_MODEL)
            w_out = fox_w_out[j]
            b_f = _lane_row(fox_b_f[j])
            qg = jnp.tile(fox_q_gain[j], 2).reshape(1, LANES) * (FOX_HEAD_DIM ** -0.5 * LOG2E)
            kg = jnp.tile(fox_k_gain[j], 2).reshape(1, LANES)
            gate_m, qx_m, kx_m, vx_m, c_m = _fox_proj(hm, 1, norm_mix[i], w_main, w_small, b_f, qg, kg,
                                                      jnp.zeros((1, LANES), f32), fox_tables)
            gate_r, qx_r, kx_r, vx_r, c_r = _fox_proj(hr, nb, norm_mix[i], w_main, w_small, b_f, qg, kg,
                                                      c_m[0, 0, 1:2, :], fox_tables)
            groups = FOX_HEADS // FOX_STEP_HEADS
            o_m = _fox_attn(qx_m, kx_m, vx_m, kx_m, vx_m, gate_m.reshape(1, N_META, D_MODEL),
                            jnp.zeros((1, groups, 1), jnp.int32), True)
            o_r = _fox_attn(qx_r, kx_r, vx_r, kx_m, vx_m, gate_r.reshape(nb, seq, D_MODEL),
                            _fox_first_block(c_r, fox_q_gain[j], fox_k_gain[j]), False)
        elif kind == 1:
            w_main, w_small = _split_in_proj(gla_w_in[j], 2 * GLA_QK + 2 * GLA_V)
            w_out = gla_w_out[j]
            main_m, small_m = _norm_proj(hm, norm_mix[i], w_main, w_small)
            main_r, small_r = _norm_proj(hr, norm_mix[i], w_main, w_small)
            w2 = jnp.pad(gla_w_alpha2[j], ((0, LANES - gla_w_alpha2.shape[1]), (0, 0)))
            ba = gla_b_alpha[j].reshape(1, GLA_QK)
            og = gla_o_gain[j].reshape(1, GLA_DV)
            s0 = jnp.zeros((1, GLA_HEADS, GLA_DV, GLA_DK), f32)
            o_m, s_m = _gla_mix(main_m.reshape(1, N_META, -1), small_m.reshape(1, N_META, LANES),
                                w2, ba, og, s0, N_META)
            o_r, _ = _gla_mix(main_r.reshape(nb, seq, -1), small_r.reshape(nb, seq, LANES),
                              w2, ba, og, s_m, GLA_CHUNK)
        else:
            w_main, w_small = _split_in_proj(gdn_w_in[j], GDN_QKV + GDN_HEADS * GDN_DV)
            w_out = gdn_w_out[j]
            main_m, small_m = _norm_proj(hm, norm_mix[i], w_main, w_small)
            main_r, small_r = _norm_proj(hr, norm_mix[i], w_main, w_small)
            cw = gdn_conv_w[j].reshape(GDN_CONV, GDN_QKV)
            alog = _lane_row(gdn_a_log[j])
            dtb = _lane_row(gdn_dt_bias[j])
            og = gdn_o_gain[j].reshape(1, GDN_DV)
            s0 = jnp.zeros((1, GDN_HEADS, GDN_DK, GDN_DV), f32)
            o_m, s_m = _gdn_mix(main_m.reshape(1, N_META, -1), small_m.reshape(1, N_META, LANES),
                                cw, alog, dtb, og, jnp.zeros((8, GDN_QKV), f32), s0, N_META)
            tail = main_m[N_META - 8:, :GDN_QKV].astype(f32)
            o_r, _ = _gdn_mix(main_r.reshape(nb, seq, -1), small_r.reshape(nb, seq, LANES),
                              cw, alog, dtb, og, tail, s_m, GDN_CHUNK)

        wo = w_out.astype(bf16)
        wgu = w_gate_up[i].astype(bf16)
        wd = w_down[i].astype(bf16)
        hm = _mix_ffn(hm, o_m.reshape(N_META, D_MODEL), wo, norm_ffn[i], wgu, wd)
        hr = _mix_ffn(hr, o_r.reshape(nb * seq, D_MODEL), wo, norm_ffn[i], wgu, wd)

    return hr.reshape(nb, seq, D_MODEL)
```

```python
import functools
import math

import jax
import jax.numpy as jnp
from jax import lax
from jax.experimental import pallas as pl
from jax.experimental.pallas import tpu as pltpu

f32 = jnp.float32
bf16 = jnp.bfloat16

D_MODEL = 1024
N_META = 16
NORM_EPS = 1e-6
LANES = 128
SUB_BLOCK = 16
LOG2E = math.log2(math.e)

FOX_HEADS = 16
FOX_HEAD_DIM = 64
FOX_BLOCK = 256
FOX_STEP_HEADS = 8
FOX_ZERO_EXP = 160.0
FOX_V_ROWS = FOX_HEAD_DIM + 16
FOX_BIAS_LANE = FOX_HEAD_DIM

GLA_HEADS = 4
GLA_DK = 128
GLA_DV = 256
GLA_QK = GLA_HEADS * GLA_DK
GLA_V = GLA_HEADS * GLA_DV
GLA_GATE_NORMALIZER = 16.0
GLA_CHUNK = 64
GLA_STEP_CHUNKS = 4

GDN_HEADS = 8
GDN_DK = 128
GDN_DV = 128
GDN_CONV = 4
GDN_CHUNK = 64
GDN_QKV = 3 * GDN_HEADS * GDN_DK
GDN_GROUP = 2
GDN_STEP_CHUNKS = 4

D_FF = 2816
FFN_CHUNK = 256

ROW_TILE = 1024
VMEM_LIMIT = 56 * 2 ** 20

_HI = lax.Precision.HIGHEST


def _params(*sem):
    return pltpu.CompilerParams(dimension_semantics=sem, vmem_limit_bytes=VMEM_LIMIT)


def _resident(shape):
    nd = len(shape)
    return pl.BlockSpec(shape, lambda *_: (0,) * nd, pipeline_mode=pl.Buffered(1))


def _dot(a, b, precision=None):
    return jnp.dot(a, b, preferred_element_type=f32, precision=precision)


def _dot_nt(a, b, precision=None):
    return lax.dot_general(a, b, (((1,), (1,)), ((), ())), preferred_element_type=f32,
                           precision=precision)


def _dot_tn(a, b, precision=None):
    return lax.dot_general(a, b, (((0,), (0,)), ((), ())), preferred_element_type=f32,
                           precision=precision)


def _bdot(a, b):
    return _dot(a.astype(bf16), b.astype(bf16))


def _sigmoid(x):
    return 1.0 / (1.0 + jnp.exp(-x))


def _softplus(x):
    return jnp.maximum(x, 0.0) + jnp.log(1.0 + jnp.exp(-jnp.abs(x)))


def _log_sigmoid(x):
    return -_softplus(-x)


def _rmsnorm(x, g):
    return x * lax.rsqrt(jnp.mean(x * x, axis=-1, keepdims=True) + NORM_EPS) * g


def _tril(n, k=0):
    r = lax.broadcasted_iota(jnp.int32, (n, n), 0)
    c = lax.broadcasted_iota(jnp.int32, (n, n), 1)
    return r + k >= c


def _mask_dot(mask, x):
    hi = x.astype(bf16)
    rest = x - hi.astype(f32)
    mid = rest.astype(bf16)
    low = (rest - mid.astype(f32)).astype(bf16)
    n = x.shape[1]
    out = _dot(mask.astype(bf16), jnp.concatenate([hi, mid, low], axis=1))
    return out[:, :n] + out[:, n:2 * n] + out[:, 2 * n:]


def _chunk_cumsum(x, chunk):
    pos = lax.broadcasted_iota(jnp.int32, (x.shape[0], 1), 0) % chunk
    d = 1
    while d < chunk:
        x = x + jnp.where(pos >= d, pltpu.roll(x, d, axis=0), 0.0)
        d *= 2
    return x


def _norm_proj_kernel(h_ref, g_ref, w_ref, ws_ref, o_ref, os_ref, *, tn):
    y = _rmsnorm(h_ref[...], g_ref[...]).astype(bf16)
    os_ref[...] = _dot(y, ws_ref[...])
    for j in range(w_ref.shape[1] // tn):
        o_ref[:, j * tn:(j + 1) * tn] = _dot(y, w_ref[:, j * tn:(j + 1) * tn]).astype(bf16)


def _norm_proj(h, gain, w_main, w_small):
    m = h.shape[0]
    tm = min(ROW_TILE, m)
    n = w_main.shape[1]
    return pl.pallas_call(
        functools.partial(_norm_proj_kernel, tn=512),
        grid=(m // tm,),
        in_specs=[pl.BlockSpec((tm, D_MODEL), lambda i: (i, 0)),
                  _resident((1, D_MODEL)), _resident(w_main.shape), _resident(w_small.shape)],
        out_specs=[pl.BlockSpec((tm, n), lambda i: (i, 0)),
                   pl.BlockSpec((tm, LANES), lambda i: (i, 0))],
        out_shape=[jax.ShapeDtypeStruct((m, n), bf16), jax.ShapeDtypeStruct((m, LANES), f32)],
        compiler_params=_params("parallel"),
        name="norm_proj",
    )(h, gain.reshape(1, D_MODEL), w_main, w_small)


def _mix_ffn_kernel(h_ref, o_ref, wo_ref, g_ref, wgu_ref, wd_ref, out_ref):
    hmid = h_ref[...] + _dot(o_ref[...], wo_ref[...])
    y = _rmsnorm(hmid, g_ref[...]).astype(bf16)
    out_ref[...] = hmid
    for f in range(D_FF // FFN_CHUNK):
        lo = f * FFN_CHUNK
        gt = _dot(y, wgu_ref[:, lo:lo + FFN_CHUNK])
        up = _dot(y, wgu_ref[:, D_FF + lo:D_FF + lo + FFN_CHUNK])
        act = (gt * _sigmoid(gt) * up).astype(bf16)
        out_ref[...] += _dot(act, wd_ref[lo:lo + FFN_CHUNK, :])


def _mix_ffn(h, o, w_out, gain, w_gate_up, w_down):
    m = h.shape[0]
    tm = min(ROW_TILE, m)
    row = lambda i: (i, 0)
    return pl.pallas_call(
        _mix_ffn_kernel,
        grid=(m // tm,),
        in_specs=[pl.BlockSpec((tm, D_MODEL), row), pl.BlockSpec((tm, D_MODEL), row),
                  _resident(w_out.shape), _resident((1, D_MODEL)),
                  _resident(w_gate_up.shape), _resident(w_down.shape)],
        out_specs=pl.BlockSpec((tm, D_MODEL), row),
        out_shape=jax.ShapeDtypeStruct((m, D_MODEL), f32),
        compiler_params=_params("parallel"),
        name="mix_ffn",
    )(h, o, w_out, gain.reshape(1, D_MODEL), w_gate_up, w_down)


def _fox_bias_tables():
    r = jnp.arange(3 * LANES)[:, None]
    lane = jnp.arange(LANES)[None, :]
    piece, src_head = r // LANES, r % LANES
    place_k = ((src_head < FOX_HEADS) & (lane == FOX_BIAS_LANE + 3 * src_head + piece)).astype(bf16)
    head = jnp.arange(FOX_HEADS)[:, None]
    first = FOX_BIAS_LANE + 3 * head
    const_q = -((lane >= first) & (lane < first + 3)).astype(f32)
    r2 = jnp.arange(2 * LANES)[:, None] % LANES
    head_sum = ((r2 // FOX_HEAD_DIM) == (lane // FOX_HEAD_DIM)).astype(bf16)
    return place_k, const_q, head_sum


def _fox_proj_kernel(h_ref, g_ref, w_ref, ws_ref, bf_ref, qg_ref, kg_ref, c0_ref, pk_ref, cq_ref, hs_ref,
                     gate_ref, qx_ref, kx_ref, vx_ref, cedge_ref, carry_ref, ext_ref, qkv_ref,
                     *, tp, tiles_per_seq):
    tm = h_ref.shape[0]

    @pl.when(pl.program_id(0) % tiles_per_seq == 0)
    def _():
        carry_ref[...] = c0_ref[...]

    y = _rmsnorm(h_ref[...], g_ref[...]).astype(bf16)
    f = _dot(y, ws_ref[...])
    tril = _tril(tp).astype(f32)
    for blk in range(tm // tp):
        rows = slice(blk * tp, (blk + 1) * tp)
        log_f = _log_sigmoid(f[rows] + bf_ref[...])
        cum = _mask_dot(tril, log_f) + carry_ref[...]
        carry_ref[...] = cum[tp - 1:tp, :]
        cedge_ref[blk, 0:1, :] = cum[0:1, :]
        cedge_ref[blk, 1:2, :] = cum[tp - 1:tp, :]
        c2 = cum * LOG2E
        hi = c2.astype(bf16)
        r1 = c2 - hi.astype(f32)
        mid = r1.astype(bf16)
        low = (r1 - mid.astype(f32)).astype(bf16)
        x3 = jnp.concatenate([hi, mid, low], axis=1)
        ext_ref[rows, :] = _dot(x3, pk_ref[...])

    lo = lax.broadcasted_iota(jnp.int32, (tm, LANES), 1) < FOX_HEAD_DIM

    def head_norm(x, g):
        x2 = x * x
        hi2 = x2.astype(bf16)
        lo2 = (x2 - hi2.astype(f32)).astype(bf16)
        ss = _dot(jnp.concatenate([hi2, lo2], axis=1), hs_ref[...])
        return x * lax.rsqrt(ss * (1.0 / FOX_HEAD_DIM) + NORM_EPS) * g

    ones = jnp.ones((FOX_V_ROWS - FOX_HEAD_DIM, tp), bf16)
    wide = 2 * LANES
    n_chunks = D_MODEL // wide

    def project(c):
        for part in range(3):
            qkv_ref[c % 2, part] = _dot(y, w_ref[:, part * D_MODEL + c * wide:part * D_MODEL + (c + 1) * wide])
        gate_ref[:, c * wide:(c + 1) * wide] = _dot(
            y, w_ref[:, 3 * D_MODEL + c * wide:3 * D_MODEL + (c + 1) * wide]).astype(bf16)

    project(0)
    for c in range(n_chunks):
        if c + 1 < n_chunks:
            project(c + 1)
        for pp in range(2):
            sl = slice(pp * LANES, (pp + 1) * LANES)
            qn = head_norm(qkv_ref[c % 2, 0, :, sl], qg_ref[...])
            kn = head_norm(qkv_ref[c % 2, 1, :, sl], kg_ref[...])
            halves = ((qn, kn), (pltpu.roll(qn, FOX_HEAD_DIM, axis=1), pltpu.roll(kn, FOX_HEAD_DIM, axis=1)))
            for hh, (qv, kv) in enumerate(halves):
                h = 2 * (2 * c + pp) + hh
                qx_ref[h] = jnp.where(lo, qv, cq_ref[h:h + 1, :]).astype(bf16)
                kx_ref[h] = jnp.where(lo, kv, ext_ref[...]).astype(bf16)
            for blk in range(tm // tp):
                vt = qkv_ref[c % 2, 2, blk * tp:(blk + 1) * tp, sl].T
                for hh in range(2):
                    h = 2 * (2 * c + pp) + hh
                    vx_ref[h, blk, 0:FOX_HEAD_DIM, :] = vt[hh * FOX_HEAD_DIM:(hh + 1) * FOX_HEAD_DIM].astype(bf16)
                    vx_ref[h, blk, FOX_HEAD_DIM:FOX_V_ROWS, :] = ones


def _fox_proj(h, nb, gain, w_main, w_small, b_f, q_gain, k_gain, c0, tables):
    m = h.shape[0]
    s = m // nb
    tm = min(ROW_TILE, s)
    tp = min(FOX_BLOCK, s)
    tps = s // tm
    head_blk = pl.BlockSpec((None, FOX_HEADS, tm, LANES), lambda i: (i // tps, 0, i % tps, 0))
    return pl.pallas_call(
        functools.partial(_fox_proj_kernel, tp=tp, tiles_per_seq=tps),
        grid=(m // tm,),
        in_specs=[pl.BlockSpec((tm, D_MODEL), lambda i: (i, 0)), _resident((1, D_MODEL)),
                  _resident(w_main.shape), _resident(w_small.shape),
                  _resident((1, LANES)), _resident((1, LANES)), _resident((1, LANES)),
                  _resident((1, LANES))] + [_resident(t.shape) for t in tables],
        out_specs=[pl.BlockSpec((tm, D_MODEL), lambda i: (i, 0)), head_blk, head_blk,
                   pl.BlockSpec((None, FOX_HEADS, tm // tp, FOX_V_ROWS, tp), lambda i: (i // tps, 0, i % tps, 0, 0)),
                   pl.BlockSpec((None, tm // tp, 2, LANES), lambda i: (i // tps, i % tps, 0, 0))],
        out_shape=[jax.ShapeDtypeStruct((m, D_MODEL), bf16),
                   jax.ShapeDtypeStruct((nb, FOX_HEADS, s, LANES), bf16),
                   jax.ShapeDtypeStruct((nb, FOX_HEADS, s, LANES), bf16),
                   jax.ShapeDtypeStruct((nb, FOX_HEADS, s // tp, FOX_V_ROWS, tp), bf16),
                   jax.ShapeDtypeStruct((nb, s // tp, 2, LANES), f32)],
        scratch_shapes=[pltpu.VMEM((1, LANES), f32), pltpu.VMEM((tm, LANES), f32),
                        pltpu.VMEM((2, 3, tm, 2 * LANES), f32)],
        compiler_params=_params("arbitrary"),
        name="fox_proj",
    )(h, gain.reshape(1, D_MODEL), w_main, w_small, b_f, q_gain, k_gain, c0, *tables)


def _fox_attn_kernel(start_ref, qx_ref, kx_ref, vx_ref, kxm_ref, vxm_ref, gate_ref, o_ref, m_ref, acc_ref,
                     s_ref, mx_ref, *, bq, nh, meta_only):
    qi = pl.program_id(2)
    heads = range(nh)
    visible = (lax.broadcasted_iota(jnp.int32, (bq, bq), 0)
               <= lax.broadcasted_iota(jnp.int32, (bq, bq), 1))

    def scores(j, diagonal):
        start = pl.multiple_of(j * bq, bq)
        ss = [_dot_nt(kx_ref[h, pl.ds(start, bq), :], qx_ref[h]) for h in heads]
        if diagonal:
            ss = [jnp.where(visible, s, -jnp.inf) for s in ss]
        return tuple(ss), tuple(jnp.max(s, axis=0, keepdims=True) for s in ss)

    def stage(blk):
        ss, mx = blk
        for h in heads:
            s_ref[h] = ss[h]
            mx_ref[h] = mx[h]

    sm = [_dot_nt(kxm_ref[h], qx_ref[h]) for h in heads]
    if meta_only:
        sm = [jnp.where(visible, s, -jnp.inf) for s in sm]

    if not meta_only:
        stage(scores(qi, True))

    ms = [jnp.max(s, axis=0, keepdims=True) for s in sm]
    ps = [jnp.exp2(s - m).astype(bf16) for s, m in zip(sm, ms)]
    for h in heads:
        m_ref[h] = ms[h]
        acc_ref[h] = _dot(vxm_ref[h], ps[h])

    if not meta_only:
        def consume(j):
            m_old = [m_ref[h] for h in heads]
            m_new = [jnp.maximum(m_old[h], mx_ref[h]) for h in heads]
            p = [jnp.exp2(s_ref[h] - m_new[h]).astype(bf16) for h in heads]
            pv = [_dot(vx_ref[h, j], p[h]) for h in heads]
            for h in heads:
                acc_ref[h] = jnp.exp2(m_old[h] - m_new[h]) * acc_ref[h] + pv[h]
                m_ref[h] = m_new[h]

        def body(j, j_prev):
            nxt = scores(j, False)
            consume(j_prev)
            stage(nxt)
            return j

        first = start_ref[pl.program_id(0), pl.program_id(1), qi]
        consume(lax.fori_loop(first, qi, body, qi))

    for pair in range(nh // 2):
        outs = []
        for h in (2 * pair, 2 * pair + 1):
            a = acc_ref[h]
            outs.append(a[0:FOX_HEAD_DIM] / a[FOX_HEAD_DIM:FOX_HEAD_DIM + 1])
        o = jnp.concatenate(outs, axis=0).T
        cols = slice(pair * LANES, (pair + 1) * LANES)
        o_ref[:, cols] = (o * _sigmoid(gate_ref[:, cols].astype(f32))).astype(bf16)


def _fox_first_block(cedge, q_gain, k_gain):
    smax = 8.0 * jnp.max(jnp.abs(q_gain)) * jnp.max(jnp.abs(k_gain)) * (LOG2E * 1.02)
    c2 = cedge[..., :FOX_HEADS] * LOG2E
    gap = c2[:, :, None, 0, :] - c2[:, None, :, 1, :]
    needed = (2.0 * smax + gap) > -FOX_ZERO_EXP
    nb, nq, nk, _ = needed.shape
    needed = needed.reshape(nb, nq, nk, FOX_HEADS // FOX_STEP_HEADS, FOX_STEP_HEADS).any(-1)
    needed = needed | (jnp.arange(nk)[None, None, :, None] >= jnp.arange(nq)[None, :, None, None])
    return jnp.argmax(needed, axis=2).astype(jnp.int32).transpose(0, 2, 1)


def _fox_attn(qx, kx, vx, kx_meta, vx_meta, gate, first_block, meta_only):
    nb, _, s, _ = qx.shape
    bq = min(FOX_BLOCK, s)
    nh = FOX_STEP_HEADS
    w = nh // 2 * LANES
    return pl.pallas_call(
        functools.partial(_fox_attn_kernel, bq=bq, nh=nh, meta_only=meta_only),
        grid=(nb, FOX_HEADS // nh, s // bq),
        in_specs=[pl.BlockSpec(memory_space=pltpu.SMEM),
                  pl.BlockSpec((None, nh, bq, LANES), lambda b, p, i: (b, p, i, 0)),
                  pl.BlockSpec((None, nh, s, LANES), lambda b, p, i: (b, p, 0, 0)),
                  pl.BlockSpec((None, nh, s // bq, FOX_V_ROWS, bq), lambda b, p, i: (b, p, 0, 0, 0)),
                  pl.BlockSpec((None, nh, N_META, LANES), lambda b, p, i: (0, p, 0, 0)),
                  pl.BlockSpec((None, nh, None, FOX_V_ROWS, N_META), lambda b, p, i: (0, p, 0, 0, 0)),
                  pl.BlockSpec((None, bq, w), lambda b, p, i: (b, i, p))],
        out_specs=pl.BlockSpec((None, bq, w), lambda b, p, i: (b, i, p)),
        out_shape=jax.ShapeDtypeStruct((nb, s, D_MODEL), bf16),
        scratch_shapes=[pltpu.VMEM((nh, 1, bq), f32), pltpu.VMEM((nh, FOX_V_ROWS, bq), f32),
                        pltpu.VMEM((nh, bq, bq), f32), pltpu.VMEM((nh, 1, bq), f32)],
        compiler_params=_params("parallel", "parallel", "arbitrary"),
        name="fox_attn",
    )(first_block, qx, kx, vx, kx_meta, vx_meta, gate)


def _gla_kernel(q_ref, k_ref, v_ref, r_ref, a_ref, w2_ref, ba_ref, og_ref, s0_ref,
                o_ref, sout_ref, st_ref, b_sc, k_sc, *, C, nc):
    c = pl.program_id(1)
    T = nc * C

    @pl.when(c == 0)
    def _():
        st_ref[...] = s0_ref[...]

    g = _log_sigmoid(_dot(a_ref[...], w2_ref[...], _HI) + ba_ref[...]) * (1.0 / GLA_GATE_NORMALIZER)
    b_sc[...] = _chunk_cumsum(g, C)
    k_sc[...] = k_ref[...].astype(f32)

    nb = C // SUB_BLOCK
    trow = lax.broadcasted_iota(jnp.int32, (SUB_BLOCK, 1), 0)
    lane = lax.broadcasted_iota(jnp.int32, (SUB_BLOCK, SUB_BLOCK), 1)

    units = [(ck, h) for ck in range(nc) for h in range(GLA_HEADS)]
    bhs, qhs, khs, vhs, a_diag, a_off = [], [], [], [], [], []
    for ck, h in units:
        tok = slice(ck * C, (ck + 1) * C)
        sl = slice(h * GLA_DK, (h + 1) * GLA_DK)
        bh = b_sc[tok, sl]
        qh = q_ref[tok, sl].astype(f32) * (GLA_DK ** -0.5)
        kh = k_sc[tok, sl]
        diag, off = [], []
        for i in range(nb):
            r0 = i * SUB_BLOCK
            bi = bh[r0:r0 + SUB_BLOCK]
            qi = qh[r0:r0 + SUB_BLOCK]
            ad = jnp.zeros((SUB_BLOCK, SUB_BLOCK), f32)
            for s in range(SUB_BLOCK):
                row = ck * C + r0 + s
                bs = b_sc[row:row + 1, sl]
                ks = k_sc[row:row + 1, sl]
                e = jnp.exp(jnp.where(trow >= s, bi - bs, -jnp.inf))
                col = jnp.sum(qi * ks * e, axis=-1, keepdims=True)
                ad = jnp.where(lane == s, col, ad)
            diag.append(ad.astype(bf16))
            if i > 0:
                ref = bi[0:1, :]
                q_rel = (qi * jnp.exp(bi - ref)).astype(bf16)
                k_rel = (kh[:r0] * jnp.exp(ref - bh[:r0])).astype(bf16)
                off.append(_dot_nt(q_rel, k_rel).astype(bf16))
        bhs.append(bh)
        qhs.append(qh)
        khs.append(kh)
        vhs.append(v_ref[tok, h * GLA_DV:(h + 1) * GLA_DV])
        a_diag.append(diag)
        a_off.append(off)

    o_intra = []
    for u in range(len(units)):
        rows = []
        for i in range(nb):
            r0 = i * SUB_BLOCK
            o_i = _dot(a_diag[u][i], vhs[u][r0:r0 + SUB_BLOCK])
            if i > 0:
                o_i = o_i + _dot(a_off[u][i - 1], vhs[u][:r0])
            rows.append(o_i)
        o_intra.append(jnp.concatenate(rows, axis=0) if nb > 1 else rows[0])

    states = [st_ref[h] for h in range(GLA_HEADS)]
    for u, (ck, h) in enumerate(units):
        bh, st = bhs[u], states[h]
        blast = bh[C - 1:C, :]
        o = o_intra[u] + _dot_nt((qhs[u] * jnp.exp(bh)).astype(bf16), st.astype(bf16))
        k_dec = (khs[u] * jnp.exp(blast - bh)).astype(bf16)
        states[h] = st * jnp.exp(blast) + _dot_tn(vhs[u], k_dec)
        tok = slice(ck * C, (ck + 1) * C)
        vs = slice(h * GLA_DV, (h + 1) * GLA_DV)
        r = r_ref[tok, vs].astype(f32)
        o_ref[tok, vs] = (_rmsnorm(o, og_ref[...]) * (r * _sigmoid(r))).astype(bf16)
    for h in range(GLA_HEADS):
        st_ref[h] = states[h]

    @pl.when(c == pl.num_programs(1) - 1)
    def _():
        sout_ref[...] = st_ref[...]


def _gla_mix(main, small, w_alpha2, b_alpha, o_gain, s0, C):
    nb, s, _ = main.shape
    nc = min(GLA_STEP_CHUNKS, s // C)
    t = nc * C
    blk = lambda w, col: pl.BlockSpec((None, t, w), lambda b, c: (b, c, col))
    state = (GLA_HEADS, GLA_DV, GLA_DK)
    return pl.pallas_call(
        functools.partial(_gla_kernel, C=C, nc=nc),
        grid=(nb, s // t),
        in_specs=[blk(GLA_QK, 0), blk(GLA_QK, 1), blk(GLA_V, 1), blk(GLA_V, 2),
                  pl.BlockSpec((None, t, LANES), lambda b, c: (b, c, 0)),
                  _resident((LANES, GLA_QK)), _resident((1, GLA_QK)), _resident((1, GLA_DV)),
                  pl.BlockSpec((None,) + state, lambda b, c: (0, 0, 0, 0))],
        out_specs=[blk(GLA_V, 0), pl.BlockSpec((None,) + state, lambda b, c: (b, 0, 0, 0))],
        out_shape=[jax.ShapeDtypeStruct((nb, s, GLA_V), bf16),
                   jax.ShapeDtypeStruct((nb,) + state, f32)],
        scratch_shapes=[pltpu.VMEM(state, f32), pltpu.VMEM((t, GLA_QK), f32),
                        pltpu.VMEM((t, GLA_QK), f32)],
        compiler_params=_params("parallel", "arbitrary"),
        name="gla_mix",
    )(main, main, main, main, small, w_alpha2, b_alpha, o_gain, s0)


def _unit_lower_inverses(lowers, n, span):
    r = lax.broadcasted_iota(jnp.int32, (n, n), 0)
    c = lax.broadcasted_iota(jnp.int32, (n, n), 1)
    eye = (r == c).astype(f32)
    same_block = (r // SUB_BLOCK) == (c // SUB_BLOCK)
    ms = [jnp.where(same_block, -lower, 0.0) for lower in lowers]
    ps = [eye + m for m in ms]
    for _ in range(3):
        ms = [_bdot(m, m) for m in ms]
        ps = [p + _bdot(p, m) for p, m in zip(ps, ms)]
    if span == SUB_BLOCK:
        return ps
    assert span // SUB_BLOCK <= 4
    es = [_bdot(p, jnp.where(same_block, 0.0, lower)) for p, lower in zip(ps, lowers)]
    e2s = [_bdot(e, e) for e in es]
    xs = [eye - e + e2 - _bdot(e, e2) for e, e2 in zip(es, e2s)]
    return [_bdot(x, p) for x, p in zip(xs, ps)]


def _gdn_kernel(x_ref, gate_ref, sm_ref, cw_ref, alog_ref, dtb_ref, og_ref, tail0_ref, s0_ref,
                o_ref, sout_ref, xbuf, s_ref, *, C, nc):
    c = pl.program_id(1)
    T = nc * C

    @pl.when(c == 0)
    def _():
        xbuf[0:8, :] = tail0_ref[...]
        s_ref[...] = s0_ref[...]

    xbuf[8:8 + T, :] = x_ref[...].astype(f32)

    def conv_silu(col):
        sl = slice(col, col + LANES)
        y = cw_ref[0:1, sl] * xbuf[5:5 + T, sl]
        for j in range(1, GDN_CONV):
            y = y + cw_ref[j:j + 1, sl] * xbuf[5 + j:5 + j + T, sl]
        return y * _sigmoid(y)

    def l2norm(x):
        return x * lax.rsqrt(jnp.sum(x * x, axis=-1, keepdims=True) + NORM_EPS)

    sm = sm_ref[...]
    g_all = -jnp.exp(alog_ref[...]) * _softplus(sm + dtb_ref[...])
    beta_all = _sigmoid(sm)
    tr = lax.broadcasted_iota(jnp.int32, (T, T), 0)
    tc = lax.broadcasted_iota(jnp.int32, (T, T), 1)
    in_chunk_tril = (((tr // C) == (tc // C)) & (tr >= tc)).astype(f32)
    b_all = _mask_dot(in_chunk_tril, g_all)
    hd = GDN_HEADS * GDN_DK
    n = GDN_GROUP * C

    rr = lax.broadcasted_iota(jnp.int32, (n, n), 0)
    cc = lax.broadcasted_iota(jnp.int32, (n, n), 1)
    strict = ((rr // C) == (cc // C)) & (rr > cc)
    eye = (rr == cc).astype(f32)

    groups = [range(g * GDN_GROUP, (g + 1) * GDN_GROUP) for g in range(GDN_HEADS // GDN_GROUP)]
    units = [(ck, g) for ck in range(nc) for g in range(len(groups))]
    rows = [slice(i * C, (i + 1) * C) for i in range(GDN_GROUP)]

    def stack(per_head):
        return [jnp.concatenate([per_head[h][ck * C:(ck + 1) * C] for h in groups[g]], axis=0)
                for ck, g in units]

    every = range(GDN_HEADS)
    q_st = stack([l2norm(conv_silu(h * GDN_DK)) * (GDN_DK ** -0.5) for h in every])
    k_st = stack([l2norm(conv_silu(hd + h * GDN_DK)) for h in every])
    v_st = stack([conv_silu(2 * hd + h * GDN_DV) for h in every])
    b_st = stack([b_all[:, h:h + 1] for h in every])
    beta_st = stack([beta_all[:, GDN_HEADS + h:GDN_HEADS + h + 1] for h in every])
    b_row = [jnp.broadcast_to(b, (n, LANES)).T[0:1, :] for b in b_st]

    dec = [jnp.exp(jnp.where(strict, b - br, -jnp.inf)) for b, br in zip(b_st, b_row)]
    kb_st = [k * beta for k, beta in zip(k_st, beta_st)]
    k16 = [k.astype(bf16) for k in k_st]
    lower = [_dot_nt(kb.astype(bf16), k) * d for kb, k, d in zip(kb_st, k16, dec)]
    attn = [(_dot_nt(q.astype(bf16), k) * (d + eye)).astype(bf16) for q, k, d in zip(q_st, k16, dec)]
    t_inv = _unit_lower_inverses(lower, n, C)
    e_b = [jnp.exp(b) for b in b_st]
    sol = [_bdot(t, jnp.concatenate([v * beta, kb * e], axis=1))
           for t, v, beta, kb, e in zip(t_inv, v_st, beta_st, kb_st, e_b)]
    qe_st = [(q * e).astype(bf16) for q, e in zip(q_st, e_b)]

    states = [s_ref[h] for h in every]
    for u, (ck, g) in enumerate(units):
        heads = groups[g]
        s16 = [states[h].astype(bf16) for h in heads]
        v_new = jnp.concatenate([sol[u][rows[i], :GDN_DV] - _dot(sol[u][rows[i], GDN_DV:].astype(bf16), s16[i])
                                 for i in range(GDN_GROUP)], axis=0)
        vn16 = v_new.astype(bf16)
        o_u = _dot(attn[u], vn16) + jnp.concatenate(
            [_dot(qe_st[u][rows[i]], s16[i]) for i in range(GDN_GROUP)], axis=0)
        for i, h in enumerate(heads):
            b_h = b_st[u][rows[i]]
            b_last = b_h[C - 1:C, :]
            k_dec = (k_st[u][rows[i]] * jnp.exp(b_last - b_h)).astype(bf16)
            states[h] = states[h] * jnp.exp(b_last) + _dot_tn(k_dec, vn16[rows[i]])
            vs = slice(h * GDN_DV, (h + 1) * GDN_DV)
            tok = slice(ck * C, (ck + 1) * C)
            gt = gate_ref[tok, vs].astype(f32)
            o_ref[tok, vs] = (_rmsnorm(o_u[rows[i]], og_ref[...]) * (gt * _sigmoid(gt))).astype(bf16)
    for h in every:
        s_ref[h] = states[h]

    xbuf[0:8, :] = xbuf[T:T + 8, :]

    @pl.when(c == pl.num_programs(1) - 1)
    def _():
        sout_ref[...] = s_ref[...]


def _gdn_mix(main, small, conv_w, a_log, dt_bias, o_gain, tail0, s0, C):
    nb, s, _ = main.shape
    state = (GDN_HEADS, GDN_DK, GDN_DV)
    nc = min(GDN_STEP_CHUNKS, s // C)
    t = nc * C
    return pl.pallas_call(
        functools.partial(_gdn_kernel, C=C, nc=nc),
        grid=(nb, s // t),
        in_specs=[pl.BlockSpec((None, t, GDN_QKV), lambda b, c: (b, c, 0)),
                  pl.BlockSpec((None, t, D_MODEL), lambda b, c: (b, c, GDN_QKV // D_MODEL)),
                  pl.BlockSpec((None, t, LANES), lambda b, c: (b, c, 0)),
                  _resident((GDN_CONV, GDN_QKV)), _resident((1, LANES)), _resident((1, LANES)),
                  _resident((1, GDN_DV)), _resident((8, GDN_QKV)),
                  pl.BlockSpec((None,) + state, lambda b, c: (0, 0, 0, 0))],
        out_specs=[pl.BlockSpec((None, t, D_MODEL), lambda b, c: (b, c, 0)),
                   pl.BlockSpec((None,) + state, lambda b, c: (b, 0, 0, 0))],
        out_shape=[jax.ShapeDtypeStruct((nb, s, D_MODEL), bf16),
                   jax.ShapeDtypeStruct((nb,) + state, f32)],
        scratch_shapes=[pltpu.VMEM((8 + t, GDN_QKV), f32), pltpu.VMEM(state, f32)],
        compiler_params=_params("parallel", "arbitrary"),
        name="gdn_mix",
    )(main, main, small, conv_w, a_log, dt_bias, o_gain, tail0, s0)


def _split_in_proj(w_in, n_main):
    w_small = jnp.pad(w_in[:, n_main:], ((0, 0), (0, LANES - (w_in.shape[1] - n_main))))
    return w_in[:, :n_main].astype(bf16), w_small.astype(bf16)


def _lane_row(v, width=LANES, offset=0):
    return jnp.pad(v.astype(f32), (offset, width - offset - v.shape[0])).reshape(1, width)


def kernel(x, meta_tokens, norm_mix, norm_ffn, w_gate_up, w_down, fox_w_in, fox_b_f, fox_q_gain, fox_k_gain, fox_w_out, gla_w_in, gla_w_alpha2, gla_b_alpha, gla_o_gain, gla_w_out, gdn_w_in, gdn_conv_w, gdn_a_log, gdn_dt_bias, gdn_o_gain, gdn_w_out):
    nb, seq, _ = x.shape
    depth = norm_mix.shape[0]
    hr = x.reshape(nb * seq, D_MODEL)
    hm = meta_tokens.astype(x.dtype)
    fox_tables = _fox_bias_tables()

    for i in range(depth):
        kind, j = i % 3, i // 3
        if kind == 0:
            w_main, w_small = _split_in_proj(fox_w_in[j], 4 * D_MODEL)
            w_out = fox_w_out[j]
            b_f = _lane_row(fox_b_f[j])
            qg = jnp.tile(fox_q_gain[j], 2).reshape(1, LANES) * (FOX_HEAD_DIM ** -0.5 * LOG2E)
            kg = jnp.tile(fox_k_gain[j], 2).reshape(1, LANES)
            gate_m, qx_m, kx_m, vx_m, c_m = _fox_proj(hm, 1, norm_mix[i], w_main, w_small, b_f, qg, kg,
                                                      jnp.zeros((1, LANES), f32), fox_tables)
            gate_r, qx_r, kx_r, vx_r, c_r = _fox_proj(hr, nb, norm_mix[i], w_main, w_small, b_f, qg, kg,
                                                      c_m[0, 0, 1:2, :], fox_tables)
            groups = FOX_HEADS // FOX_STEP_HEADS
            o_m = _fox_attn(qx_m, kx_m, vx_m, kx_m, vx_m, gate_m.reshape(1, N_META, D_MODEL),
                            jnp.zeros((1, groups, 1), jnp.int32), True)
            o_r = _fox_attn(qx_r, kx_r, vx_r, kx_m, vx_m, gate_r.reshape(nb, seq, D_MODEL),
                            _fox_first_block(c_r, fox_q_gain[j], fox_k_gain[j]), False)
        elif kind == 1:
            w_main, w_small = _split_in_proj(gla_w_in[j], 2 * GLA_QK + 2 * GLA_V)
            w_out = gla_w_out[j]
            main_m, small_m = _norm_proj(hm, norm_mix[i], w_main, w_small)
            main_r, small_r = _norm_proj(hr, norm_mix[i], w_main, w_small)
            w2 = jnp.pad(gla_w_alpha2[j], ((0, LANES - gla_w_alpha2.shape[1]), (0, 0)))
            ba = gla_b_alpha[j].reshape(1, GLA_QK)
            og = gla_o_gain[j].reshape(1, GLA_DV)
            s0 = jnp.zeros((1, GLA_HEADS, GLA_DV, GLA_DK), f32)
            o_m, s_m = _gla_mix(main_m.reshape(1, N_META, -1), small_m.reshape(1, N_META, LANES),
                                w2, ba, og, s0, N_META)
            o_r, _ = _gla_mix(main_r.reshape(nb, seq, -1), small_r.reshape(nb, seq, LANES),
                              w2, ba, og, s_m, GLA_CHUNK)
        else:
            w_main, w_small = _split_in_proj(gdn_w_in[j], GDN_QKV + GDN_HEADS * GDN_DV)
            w_out = gdn_w_out[j]
            main_m, small_m = _norm_proj(hm, norm_mix[i], w_main, w_small)
            main_r, small_r = _norm_proj(hr, norm_mix[i], w_main, w_small)
            cw = gdn_conv_w[j].reshape(GDN_CONV, GDN_QKV)
            alog = _lane_row(gdn_a_log[j])
            dtb = _lane_row(gdn_dt_bias[j])
            og = gdn_o_gain[j].reshape(1, GDN_DV)
            s0 = jnp.zeros((1, GDN_HEADS, GDN_DK, GDN_DV), f32)
            o_m, s_m = _gdn_mix(main_m.reshape(1, N_META, -1), small_m.reshape(1, N_META, LANES),
                                cw, alog, dtb, og, jnp.zeros((8, GDN_QKV), f32), s0, N_META)
            tail = main_m[N_META - 8:, :GDN_QKV].astype(f32)
            o_r, _ = _gdn_mix(main_r.reshape(nb, seq, -1), small_r.reshape(nb, seq, LANES),
                              cw, alog, dtb, og, tail, s_m, GDN_CHUNK)

        wo = w_out.astype(bf16)
        wgu = w_gate_up[i].astype(bf16)
        wd = w_down[i].astype(bf16)
        hm = _mix_ffn(hm, o_m.reshape(N_META, D_MODEL), wo, norm_ffn[i], wgu, wd)
        hr = _mix_ffn(hr, o_r.reshape(nb * seq, D_MODEL), wo, norm_ffn[i], wgu, wd)

    return hr.reshape(nb, seq, D_MODEL)
```
